```python
import math
import jax, jax.numpy as jnp
from jax import lax
import numpy as np

D_MODEL = 4096
BATCH = 2
SEQ = 4096
DEPTH = 4
DEC_BATCH = 8
DEC_SEQ = 16
PAST_LEN = 1024

CHUNK = 64
A_HEADS = 12
A_QK_DIM = 64
A_V_DIM = 2 * A_QK_DIM
A_WIDTH = A_HEADS * A_V_DIM
B_HEADS = 12
B_HEAD_DIM = 128
B_WIDTH = B_HEADS * B_HEAD_DIM
B_LEFT_CHUNKS = 8
B_WIN = B_LEFT_CHUNKS * CHUNK
B_MAX_DIST = 128
C_HEADS = 8
C_DK = 128
C_DV = 128
C_WIDTH = C_HEADS * C_DV
D_FF = 4 * D_MODEL
T5_BUCKETS = 32
T5_MAX_DIST = 128
Q_BLOCK = 128
DEEPNORM_ALPHA = (2 * DEPTH) ** 0.25
DEEPNORM_BETA = (8 * DEPTH) ** -0.25
LN_EPS = 1e-5
RMS_EPS = 1e-6
NEG = -1e30
N_IN_COLS = 3 * A_WIDTH + 3 * B_WIDTH + 4 * C_WIDTH + 3 * D_MODEL

kernel_name = 'hybrid_streaming_encoder_step'


def layer_norm(x, g, b):
    xf = x.astype(jnp.float32)
    mu = jnp.mean(xf, axis=-1, keepdims=True)
    var = jnp.mean(jnp.square(xf - mu), axis=-1, keepdims=True)
    return ((xf - mu) * lax.rsqrt(var + LN_EPS) * g + b).astype(x.dtype)


def rms_norm(x, g):
    xf = x.astype(jnp.float32)
    return (xf * lax.rsqrt(jnp.mean(jnp.square(xf), axis=-1, keepdims=True) + RMS_EPS) * g).astype(x.dtype)


def split_in(z):
    sizes = [A_WIDTH] * 3 + [B_WIDTH] * 3 + [C_WIDTH] * 4 + [D_MODEL] * 3
    return jnp.split(z, np.cumsum(sizes)[:-1].tolist(), axis=-1)


def t5_bucket(rel):
    half = T5_BUCKETS // 2
    n = -rel
    ret = jnp.where(n < 0, half, 0)
    n = jnp.abs(n)
    max_exact = half // 2
    nf = jnp.maximum(n, 1).astype(jnp.float32)
    large = max_exact + (jnp.log(nf / max_exact) / math.log(T5_MAX_DIST / max_exact) * (half - max_exact)).astype(jnp.int32)
    large = jnp.minimum(large, half - 1)
    return ret + jnp.where(n < max_exact, n, large)


def t5_lookup(table, qpos, kpos):
    return jnp.moveaxis(table[t5_bucket(kpos[None, :] - qpos[:, None])], -1, 0)


def rel_lookup(table, dist):
    return table[:, jnp.clip(dist, -B_MAX_DIST, B_MAX_DIST) + B_MAX_DIST]


def diff_attend(q1, q2, k1, k2, v, bias, mask, lam):
    def probs(q, k):
        s = jnp.einsum('bqhd,bkhd->bhqk', q, k).astype(jnp.float32) * (A_QK_DIM ** -0.5) + bias
        return jax.nn.softmax(jnp.where(mask, s, NEG), axis=-1)
    p = probs(q1, k1) - lam * probs(q2, k2)
    return jnp.einsum('bhqk,bkhd->bqhd', p.astype(v.dtype), v)


def diff_attn_prompt(q1, q2, k1, k2, v, t5_table, lam):
    B, S, H, _ = q1.shape
    kpos = jnp.arange(S)

    def block(i):
        start = i * Q_BLOCK
        qb1 = lax.dynamic_slice_in_dim(q1, start, Q_BLOCK, axis=1)
        qb2 = lax.dynamic_slice_in_dim(q2, start, Q_BLOCK, axis=1)
        qpos = start + jnp.arange(Q_BLOCK)
        bias = t5_lookup(t5_table, qpos, kpos)
        mask = (kpos[None, :] // CHUNK) <= (qpos[:, None] // CHUNK)
        return diff_attend(qb1, qb2, k1, k2, v, bias, mask, lam)

    out = lax.map(block, jnp.arange(S // Q_BLOCK))
    return jnp.moveaxis(out, 0, 1).reshape(B, S, H, -1)


def band_gather(t):
    B, S, H, d = t.shape
    nC = S // CHUNK
    padded = jnp.pad(t, ((0, 0), (B_WIN, 0), (0, 0), (0, 0))).reshape(B, nC + B_LEFT_CHUNKS, CHUNK, H, d)
    return jnp.concatenate([padded[:, j:j + nC] for j in range(B_LEFT_CHUNKS + 1)], axis=2)


def band_attend(q, k, v, bias, mask):
    s = jnp.einsum('bnqhd,bnkhd->bnhqk', q, k).astype(jnp.float32) * (B_HEAD_DIM ** -0.5) + bias
    p = jax.nn.softmax(jnp.where(mask[None, :, None], s, NEG), axis=-1)
    return jnp.einsum('bnhqk,bnkhd->bnqhd', p.astype(v.dtype), v)


def band_attn_prompt(q, k, v, rel_table):
    B, S, H, d = q.shape
    nC = S // CHUNK
    band = (B_LEFT_CHUNKS + 1) * CHUNK
    i = jnp.arange(CHUNK)
    j = jnp.arange(band)
    bias = rel_lookup(rel_table, i[:, None] + B_WIN - j[None, :])
    kpos = jnp.arange(nC)[:, None] * CHUNK - B_WIN + j[None, :]
    mask = (kpos >= 0)[:, None, :]
    o = band_attend(q.reshape(B, nC, CHUNK, H, d), band_gather(k), band_gather(v), bias, mask)
    return o.reshape(B, S, H, d)


def hgrn2_chunked(q, k, logf, v, s0):
    q, k, logf, v = (t.astype(jnp.float32) for t in (q, k, logf, v))
    B, T, H, _ = q.shape
    L = min(CHUNK, T)
    nC = T // L
    causal = jnp.tril(jnp.ones((L, L), dtype=bool))[None, :, :, None, None]

    def to_chunks(t):
        return jnp.moveaxis(t.reshape(B, nC, L, H, t.shape[-1]), 1, 0)

    def step(S, inp):
        qc, kc, fc, vc = inp
        b = jnp.cumsum(fc, axis=1)
        decay = jnp.exp(jnp.where(causal, b[:, :, None] - b[:, None, :], NEG))
        scores = jnp.einsum('bthd,bshd,btshd->bhts', qc, kc, decay)
        o = jnp.einsum('bhts,bshv->bthv', scores, vc) + jnp.einsum('bthd,bhdv->bthv', qc * jnp.exp(b), S)
        b_last = b[:, -1]
        S = jnp.exp(b_last)[..., None] * S + jnp.einsum('bshd,bshv->bhdv', kc * jnp.exp(b_last[:, None] - b), vc)
        return S, o

    S, o = lax.scan(step, s0.astype(jnp.float32), (to_chunks(q), to_chunks(k), to_chunks(logf), to_chunks(v)))
    return jnp.moveaxis(o, 0, 1).reshape(B, T, H, -1), S


def adaln(c, w, b):
    mod = jnp.einsum('bd,dn->bn', jax.nn.silu(c), w) + b
    return jnp.split(mod[:, None, :], 6, axis=-1)


def trunk_layer(x, c, lp, layer_idx, past):
    B, T, _ = x.shape
    sh1, sc1, g1, sh2, sc2, g2 = adaln(c, lp['w_ada'], lp['b_ada'])
    h = x * (1 + sc1) + sh1
    z = jnp.einsum('btd,dn->btn', h, lp['w_in'])
    qa, ka, va, qb, kb, vb, qc, fc, ic, gc, gate_a, gate_b, gate_c = split_in(z)

    qa = qa.reshape(B, T, A_HEADS, 2, A_QK_DIM)
    ka = ka.reshape(B, T, A_HEADS, 2 * A_QK_DIM)
    va = va.reshape(B, T, A_HEADS, A_V_DIM)
    lam_init = 0.8 - 0.6 * math.exp(-0.3 * layer_idx)
    lq1, lk1, lq2, lk2 = lp['diff_lambda']
    lam = jnp.exp(jnp.sum(lq1 * lk1)) - jnp.exp(jnp.sum(lq2 * lk2)) + lam_init
    qb = qb.reshape(B, T, B_HEADS, B_HEAD_DIM)
    kb = kb.reshape(B, T, B_HEADS, B_HEAD_DIM)
    vb = vb.reshape(B, T, B_HEADS, B_HEAD_DIM)
    if past is None:
        oa = diff_attn_prompt(qa[..., 0, :], qa[..., 1, :], ka[..., :A_QK_DIM], ka[..., A_QK_DIM:], va, lp['t5_bias'], lam)
        ob = band_attn_prompt(qb, kb, vb, lp['b_rel_bias'])
        s0 = jnp.zeros((B, C_HEADS, C_DK, C_DV), jnp.float32)
        n_b_rows = min(B_WIN, T)
    else:
        ka_past, va_past, kb_past, vb_past, s0 = past
        P = ka_past.shape[1]
        qpos = P + jnp.arange(T)
        k_all = jnp.concatenate([ka_past, ka], axis=1)
        v_all = jnp.concatenate([va_past, va], axis=1)
        kpos = jnp.arange(P + T)
        mask_a = (kpos[None, :] // CHUNK) <= (qpos[:, None] // CHUNK)
        oa = diff_attend(qa[..., 0, :], qa[..., 1, :], k_all[..., :A_QK_DIM], k_all[..., A_QK_DIM:], v_all,
                         t5_lookup(lp['t5_bias'], qpos, kpos), mask_a, lam)
        Pc = kb_past.shape[1]
        kbpos = jnp.concatenate([P - Pc + jnp.arange(Pc), qpos])
        qch, kch = qpos[:, None] // CHUNK, kbpos[None, :] // CHUNK
        mask_b = (kch <= qch) & (kch >= qch - B_LEFT_CHUNKS)
        kb_all = jnp.concatenate([kb_past, kb], axis=1)
        vb_all = jnp.concatenate([vb_past, vb], axis=1)
        ob = band_attend(qb[:, None], kb_all[:, None], vb_all[:, None],
                         rel_lookup(lp['b_rel_bias'], qpos[:, None] - kbpos[None, :]), mask_b[None])[:, 0]
        n_b_rows = T
    oa = (rms_norm(oa, lp['a_subln_g']) * (1 - lam_init)).reshape(B, T, A_WIDTH)
    ob = ob.reshape(B, T, B_WIDTH)

    lb = lp['c_lb'].reshape(C_HEADS, C_DK)
    zf = fc.reshape(B, T, C_HEADS, C_DK).astype(jnp.float32)
    sig = jax.nn.sigmoid(zf)
    logf = jnp.log(lb + (1 - lb) * sig)
    kc = (1 - lb) * (1 - sig)
    qcs = jax.nn.silu(qc.reshape(B, T, C_HEADS, C_DK))
    oc, s_new = hgrn2_chunked(qcs, kc, logf, ic.reshape(B, T, C_HEADS, C_DV), s0)
    oc = (rms_norm(oc, lp['c_norm_g']) * jax.nn.silu(gc.reshape(B, T, C_HEADS, C_DV)).astype(jnp.float32))
    oc = oc.reshape(B, T, C_WIDTH).astype(x.dtype)

    mixed = (jax.nn.sigmoid(gate_a) * jnp.einsum('btn,nd->btd', oa, lp['w_branch_a'])
             + jax.nn.sigmoid(gate_b) * jnp.einsum('btn,nd->btd', ob, lp['w_branch_b'])
             + jax.nn.sigmoid(gate_c) * jnp.einsum('btn,nd->btd', oc, lp['w_branch_c']))
    y = jnp.einsum('btd,de->bte', mixed, lp['w_o'])
    x = layer_norm(DEEPNORM_ALPHA * x + (1 + g1) * y, lp['ln1_g'], lp['ln1_b'])

    h2 = x * (1 + sc2) + sh2
    m = jnp.einsum('btf,fd->btd', jnp.square(jax.nn.relu(jnp.einsum('btd,df->btf', h2, lp['w_up']))), lp['w_down'])
    x = layer_norm(DEEPNORM_ALPHA * x + (1 + g2) * m, lp['ln2_g'], lp['ln2_b'])
    return x, (ka, va, kb[:, T - n_b_rows:], vb[:, T - n_b_rows:], s_new)


def setup_inputs(seed: int = 0) -> dict:
    key = jax.random.key(seed)
    ks = jax.random.split(key, 28)
    f32 = jnp.float32

    def nrm(k, shape, scale):
        return scale * jax.random.normal(k, shape, f32)

    beta = DEEPNORM_BETA
    b_cache = min(B_WIN, PAST_LEN)
    col_scale = jnp.concatenate([
        jnp.ones((2 * A_WIDTH,), f32), jnp.full((A_WIDTH,), beta, f32),
        jnp.ones((2 * B_WIDTH,), f32), jnp.full((B_WIDTH,), beta, f32),
        jnp.ones((2 * C_WIDTH,), f32), jnp.full((C_WIDTH,), beta, f32),
        jnp.ones((C_WIDTH + 3 * D_MODEL,), f32)])
    return {
        'x_prompt': nrm(ks[0], (BATCH, SEQ, D_MODEL), 1.0),
        'x_sample': nrm(ks[1], (DEC_BATCH, DEC_SEQ, D_MODEL), 1.0),
        'cache_a_k': nrm(ks[2], (DEPTH, DEC_BATCH, PAST_LEN, A_HEADS, 2 * A_QK_DIM), 1.0),
        'cache_a_v': nrm(ks[3], (DEPTH, DEC_BATCH, PAST_LEN, A_HEADS, A_V_DIM), 1.0),
        'cache_b_k': nrm(ks[4], (DEPTH, DEC_BATCH, b_cache, B_HEADS, B_HEAD_DIM), 1.0),
        'cache_b_v': nrm(ks[5], (DEPTH, DEC_BATCH, b_cache, B_HEADS, B_HEAD_DIM), 1.0),
        'state_c': nrm(ks[6], (DEPTH, DEC_BATCH, C_HEADS, C_DK, C_DV), 0.5),
        'c_prompt': nrm(ks[7], (BATCH, D_MODEL), 1.0),
        'c_sample': nrm(ks[8], (DEC_BATCH, D_MODEL), 1.0),
        'w_ada': nrm(ks[9], (DEPTH, D_MODEL, 6 * D_MODEL), 0.3 * D_MODEL ** -0.5),
        'b_ada': nrm(ks[10], (DEPTH, 6 * D_MODEL), 0.01),
        'w_in': nrm(ks[11], (DEPTH, D_MODEL, N_IN_COLS), D_MODEL ** -0.5) * col_scale,
        'diff_lambda': nrm(ks[12], (DEPTH, 4, A_QK_DIM), 0.1),
        'a_subln_g': 1.0 + nrm(ks[13], (DEPTH, A_V_DIM), 0.01),
        't5_bias': nrm(ks[14], (T5_BUCKETS, A_HEADS), 0.5),
        'b_rel_bias': nrm(ks[15], (DEPTH, B_HEADS, 2 * B_MAX_DIST + 1), 0.5),
        'c_lb_param': 1.0 + nrm(ks[16], (DEPTH, C_HEADS * C_DK), 0.1),
        'c_norm_g': 1.0 + nrm(ks[17], (DEPTH, C_DV), 0.01),
        'w_branch_a': nrm(ks[18], (DEPTH, A_WIDTH, D_MODEL), beta * A_WIDTH ** -0.5),
        'w_branch_b': nrm(ks[19], (DEPTH, B_WIDTH, D_MODEL), beta * B_WIDTH ** -0.5),
        'w_branch_c': nrm(ks[20], (DEPTH, C_WIDTH, D_MODEL), beta * C_WIDTH ** -0.5),
        'w_o': nrm(ks[21], (DEPTH, D_MODEL, D_MODEL), beta * D_MODEL ** -0.5),
        'ln1_g': 1.0 + nrm(ks[22], (DEPTH, D_MODEL), 0.01),
        'ln1_b': nrm(ks[23], (DEPTH, D_MODEL), 0.01),
        'ln2_g': 1.0 + nrm(ks[24], (DEPTH, D_MODEL), 0.01),
        'ln2_b': nrm(ks[25], (DEPTH, D_MODEL), 0.01),
        'w_up': nrm(ks[26], (DEPTH, D_MODEL, D_FF), beta * D_MODEL ** -0.5),
        'w_down': nrm(ks[27], (DEPTH, D_FF, D_MODEL), beta * D_FF ** -0.5),
    }


def reference(x_prompt, x_sample, cache_a_k, cache_a_v, cache_b_k, cache_b_v, state_c, c_prompt, c_sample,
              w_ada, b_ada, w_in, diff_lambda, a_subln_g, t5_bias, b_rel_bias, c_lb_param, c_norm_g,
              w_branch_a, w_branch_b, w_branch_c, w_o, ln1_g, ln1_b, ln2_g, ln2_b, w_up, w_down):
    lb_soft = jax.nn.softmax(c_lb_param.astype(jnp.float32), axis=0)
    lb_all = jnp.cumsum(lb_soft, axis=0) - lb_soft[0]
    xp, xs = x_prompt, x_sample
    ak_p, av_p, bk_p, bv_p, sc_p = [], [], [], [], []
    ak_s, av_s, bk_s, bv_s, sc_s = [], [], [], [], []
    for l in range(DEPTH):
        lp = {'w_ada': w_ada[l], 'b_ada': b_ada[l], 'w_in': w_in[l], 'diff_lambda': diff_lambda[l],
              'a_subln_g': a_subln_g[l], 't5_bias': t5_bias, 'b_rel_bias': b_rel_bias[l], 'c_lb': lb_all[l],
              'c_norm_g': c_norm_g[l], 'w_branch_a': w_branch_a[l], 'w_branch_b': w_branch_b[l],
              'w_branch_c': w_branch_c[l], 'w_o': w_o[l], 'ln1_g': ln1_g[l], 'ln1_b': ln1_b[l],
              'ln2_g': ln2_g[l], 'ln2_b': ln2_b[l], 'w_up': w_up[l], 'w_down': w_down[l]}
        xp, (k_a, v_a, k_b, v_b, s_c) = trunk_layer(xp, c_prompt, lp, l, None)
        ak_p.append(k_a); av_p.append(v_a); bk_p.append(k_b); bv_p.append(v_b); sc_p.append(s_c)
        xs, (k_a, v_a, k_b, v_b, s_c) = trunk_layer(
            xs, c_sample, lp, l, (cache_a_k[l], cache_a_v[l], cache_b_k[l], cache_b_v[l], state_c[l]))
        ak_s.append(k_a); av_s.append(v_a); bk_s.append(k_b); bv_s.append(v_b); sc_s.append(s_c)
    return (xp, xs,
            jnp.stack(ak_p), jnp.stack(av_p), jnp.stack(bk_p), jnp.stack(bv_p), jnp.stack(sc_p),
            jnp.stack(ak_s), jnp.stack(av_s), jnp.stack(bk_s), jnp.stack(bv_s), jnp.stack(sc_s))
```

```python
import functools
import math
from typing import NamedTuple

import jax
import jax.numpy as jnp
from jax import lax
from jax.experimental import pallas as pl
from jax.experimental.pallas import tpu as pltpu

F32 = jnp.float32
BF16 = jnp.bfloat16
NEG = -1e30
LN_EPS = 1e-5
RMS_EPS = 1e-6
HEAD_LANES = 128
SUB = 16
VMEM_LIMIT_V7X = 56 * 1024 * 1024


class Cfg(NamedTuple):
    d_model: int = 4096
    batch: int = 2
    seq: int = 4096
    depth: int = 4
    dec_batch: int = 8
    dec_seq: int = 16
    past_len: int = 1024
    chunk: int = 64
    a_heads: int = 12
    a_qk: int = 64
    b_heads: int = 12
    b_left_chunks: int = 8
    b_max_dist: int = 128
    c_heads: int = 8
    t5_buckets: int = 32
    t5_max_dist: int = 128
    bm: int = 640
    bn: int = 512
    bq_a: int = 256
    bq_b: int = 256

    @property
    def a_width(self): return self.a_heads * HEAD_LANES
    @property
    def b_width(self): return self.b_heads * HEAD_LANES
    @property
    def c_width(self): return self.c_heads * HEAD_LANES
    @property
    def d_ff(self): return 4 * self.d_model
    @property
    def b_win(self): return self.b_left_chunks * self.chunk
    @property
    def n_in(self): return 3 * self.a_width + 3 * self.b_width + 4 * self.c_width + 3 * self.d_model
    @property
    def rows_p(self): return self.batch * self.seq
    @property
    def rows_s(self): return self.dec_batch * self.dec_seq
    @property
    def rows(self): return self.rows_p + self.rows_s
    @property
    def alpha(self): return (2 * self.depth) ** 0.25


def _cparams(n_axes, vmem=None):
    return pltpu.CompilerParams(dimension_semantics=("arbitrary",) * n_axes, vmem_limit_bytes=vmem)


def _dot(a, b):
    return jnp.dot(a, b, preferred_element_type=F32)


def _dot_nt(a, b):
    return lax.dot_general(a, b, (((1,), (1,)), ((), ())), preferred_element_type=F32)


def _dot_tn(a, b):
    return lax.dot_general(a, b, (((0,), (0,)), ((), ())), preferred_element_type=F32)


def _silu(x):
    return x * jax.nn.sigmoid(x)


def _cast_rows(dst_ref, src_ref, rows_per_step=512):
    n = src_ref.shape[0]
    step = min(rows_per_step, n)
    for r in range(0, n, step):
        dst_ref[r:r + step, :] = src_ref[r:r + step, :].astype(BF16)


def _adaln_kernel(c_ref, w_ref, b_ref, o_ref):
    s = _silu(c_ref[...]).astype(BF16)
    o_ref[0] = _dot(s, w_ref[0].astype(BF16)) + b_ref[0]


def adaln_all(cfg, c_all, w_ada, b_ada):
    R, D = c_all.shape
    L, _, N = w_ada.shape
    bn = cfg.bn
    return pl.pallas_call(
        _adaln_kernel,
        out_shape=jax.ShapeDtypeStruct((L, R, N), F32),
        grid=(L, N // bn),
        in_specs=[pl.BlockSpec((R, D), lambda l, j: (0, 0)),
                  pl.BlockSpec((1, D, bn), lambda l, j: (l, 0, j)),
                  pl.BlockSpec((1, 1, bn), lambda l, j: (l, 0, j))],
        out_specs=pl.BlockSpec((1, R, bn), lambda l, j: (l, 0, j)),
        compiler_params=_cparams(2, VMEM_LIMIT_V7X),
        name="adaln",
    )(c_all, w_ada, b_ada.reshape(L, 1, N))


def _rowwise_kernel(*refs, cfg, do_ln, do_mod, g_col, sc_col, sh_col):
    refs = list(refs)
    x_ref = refs.pop(0)
    y_ref = refs.pop(0) if do_ln else None
    modg_ref = refs.pop(0) if do_ln else None
    lng_ref = refs.pop(0) if do_ln else None
    lnb_ref = refs.pop(0) if do_ln else None
    modn_ref = refs.pop(0) if do_mod else None
    xo_ref = refs.pop(0) if do_ln else None
    ho_ref = refs.pop(0) if do_mod else None
    D = cfg.d_model
    bm = cfg.rows_s
    n_prompt_blocks = cfg.rows_p // bm
    blocks_per_batch = cfg.seq // bm
    i = pl.program_id(0)

    def slab(r0, nrows, brow):
        x = x_ref[r0:r0 + nrows, :]
        if do_ln:
            g = modg_ref[pl.ds(brow, 1), g_col * D:(g_col + 1) * D]
            u = cfg.alpha * x + (1.0 + g) * y_ref[r0:r0 + nrows, :]
            mu = jnp.mean(u, axis=-1, keepdims=True)
            var = jnp.mean(jnp.square(u - mu), axis=-1, keepdims=True)
            x = (u - mu) * lax.rsqrt(var + LN_EPS) * lng_ref[...] + lnb_ref[...]
            xo_ref[r0:r0 + nrows, :] = x
        if do_mod:
            sc = modn_ref[pl.ds(brow, 1), sc_col * D:(sc_col + 1) * D]
            sh = modn_ref[pl.ds(brow, 1), sh_col * D:(sh_col + 1) * D]
            ho_ref[r0:r0 + nrows, :] = (x * (1.0 + sc) + sh).astype(BF16)

    @pl.when(i < n_prompt_blocks)
    def _():
        slab(0, bm, i // blocks_per_batch)

    @pl.when(i >= n_prompt_blocks)
    def _():
        for s in range(cfg.dec_batch):
            slab(s * cfg.dec_seq, cfg.dec_seq, cfg.batch + s)


def rowwise(cfg, x, y=None, mod_gate=None, ln_g=None, ln_b=None, mod_next=None, *, g_col=0, sc_col=0, sh_col=0):
    do_ln = y is not None
    do_mod = mod_next is not None
    M, D = x.shape
    bm = cfg.rows_s
    row = pl.BlockSpec((bm, D), lambda i: (i, 0))
    full = lambda a: pl.BlockSpec(a.shape, lambda i: (0,) * a.ndim)
    args, specs, outs, out_specs = [x], [row], [], []
    if do_ln:
        ln_g = ln_g.reshape(1, D)
        ln_b = ln_b.reshape(1, D)
        args += [y, mod_gate, ln_g, ln_b]
        specs += [row, full(mod_gate), full(ln_g), full(ln_b)]
        outs.append(jax.ShapeDtypeStruct((M, D), F32))
        out_specs.append(row)
    if do_mod:
        args.append(mod_next)
        specs.append(full(mod_next))
        outs.append(jax.ShapeDtypeStruct((M, D), BF16))
        out_specs.append(row)
    res = pl.pallas_call(
        functools.partial(_rowwise_kernel, cfg=cfg, do_ln=do_ln, do_mod=do_mod,
                          g_col=g_col, sc_col=sc_col, sh_col=sh_col),
        out_shape=outs, grid=(M // bm,), in_specs=specs, out_specs=out_specs,
        compiler_params=_cparams(1, VMEM_LIMIT_V7X), name="rowwise",
    )(*args)
    return res


def _mm_kernel(x_ref, w_ref, o_ref, wbf_ref, *acc, nk, relu2):
    k = pl.program_id(1)
    i = pl.program_id(2)

    @pl.when(i == 0)
    def _():
        _cast_rows(wbf_ref, w_ref)

    d = _dot(x_ref[...], wbf_ref[...])

    def finish(r):
        if relu2:
            r = jnp.square(jnp.maximum(r, 0.0))
        o_ref[...] = r.astype(o_ref.dtype)

    if nk == 1:
        finish(d)
    else:
        acc_ref = acc[0]

        @pl.when(k == 0)
        def _():
            acc_ref[i] = d

        @pl.when(jnp.logical_and(k > 0, k < nk - 1))
        def _():
            acc_ref[i] += d

        @pl.when(k == nk - 1)
        def _():
            finish(acc_ref[i] + d)


def matmul(cfg, x, w, *, out_dtype=F32, relu2=False, bk=None, name="matmul"):
    M, K = x.shape
    _, N = w.shape
    bm, bn = cfg.bm, cfg.bn
    bk = K if bk is None else bk
    nk, nm = K // bk, M // bm
    last = nk - 1
    scratch = [pltpu.VMEM((bk, bn), BF16)]
    if nk > 1:
        scratch.append(pltpu.VMEM((nm, bm, bn), F32))
        out_map = lambda j, k, i: (jnp.where(k == last, i, 0), j)
    else:
        out_map = lambda j, k, i: (i, j)
    return pl.pallas_call(
        functools.partial(_mm_kernel, nk=nk, relu2=relu2),
        out_shape=jax.ShapeDtypeStruct((M, N), out_dtype),
        grid=(N // bn, nk, nm),
        in_specs=[pl.BlockSpec((bm, bk), lambda j, k, i: (i, k)),
                  pl.BlockSpec((bk, bn), lambda j, k, i: (k, j))],
        out_specs=pl.BlockSpec((bm, bn), out_map),
        scratch_shapes=scratch,
        compiler_params=_cparams(3, VMEM_LIMIT_V7X), name=name,
    )(x, w)


def _mix_kernel(oa_ref, ob_ref, oc_ref, ga_ref, gb_ref, gc_ref, wa_ref, wb_ref, wc_ref, o_ref, wa_s, wb_s, wc_s):
    @pl.when(pl.program_id(1) == 0)
    def _():
        _cast_rows(wa_s, wa_ref)
        _cast_rows(wb_s, wb_ref)
        _cast_rows(wc_s, wc_ref)

    mixed = (jax.nn.sigmoid(ga_ref[...]) * _dot(oa_ref[...], wa_s[...])
             + jax.nn.sigmoid(gb_ref[...]) * _dot(ob_ref[...], wb_s[...])
             + jax.nn.sigmoid(gc_ref[...]) * _dot(oc_ref[...], wc_s[...]))
    o_ref[...] = mixed.astype(o_ref.dtype)


def branch_mix(cfg, oa, ob, oc, z, wa, wb, wc):
    M = oa.shape[0]
    D = cfg.d_model
    bm, bn = cfg.bm, cfg.bn
    g0 = (3 * cfg.a_width + 3 * cfg.b_width + 4 * cfg.c_width) // bn
    gstep = D // bn
    act = lambda w: pl.BlockSpec((bm, w), lambda j, i: (i, 0))
    gate = lambda t: pl.BlockSpec((bm, bn), lambda j, i: (i, g0 + t * gstep + j))
    wgt = lambda w: pl.BlockSpec((w, bn), lambda j, i: (0, j))
    return pl.pallas_call(
        _mix_kernel,
        out_shape=jax.ShapeDtypeStruct((M, D), BF16),
        grid=(D // bn, M // bm),
        in_specs=[act(cfg.a_width), act(cfg.b_width), act(cfg.c_width), gate(0), gate(1), gate(2),
                  wgt(cfg.a_width), wgt(cfg.b_width), wgt(cfg.c_width)],
        out_specs=pl.BlockSpec((bm, bn), lambda j, i: (i, j)),
        scratch_shapes=[pltpu.VMEM((cfg.a_width, bn), BF16), pltpu.VMEM((cfg.b_width, bn), BF16),
                        pltpu.VMEM((cfg.c_width, bn), BF16)],
        compiler_params=_cparams(2, VMEM_LIMIT_V7X), name="branch_mix",
    )(oa, ob, oc, z, z, z, wa, wb, wc)


def _diff_lambda(dl_ref, lam_init):
    dl = dl_ref[...]
    s1 = jnp.sum(dl[0:1] * dl[1:2], axis=-1, keepdims=True)
    s2 = jnp.sum(dl[2:3] * dl[3:4], axis=-1, keepdims=True)
    return jnp.exp(s1) - jnp.exp(s2) + lam_init


def _split_q(q, half):
    lane = lax.broadcasted_iota(jnp.int32, q.shape, 1)
    return (jnp.where(lane < half, q, 0.0).astype(BF16), jnp.where(lane >= half, q, 0.0).astype(BF16))


def _rms(o, g):
    return o * lax.rsqrt(jnp.mean(jnp.square(o), axis=-1, keepdims=True) + RMS_EPS) * g


def _flash_step(s, v, st):
    m, l, acc = st
    m_new = jnp.maximum(m, jnp.max(s, axis=-1, keepdims=True))
    a = jnp.exp(m - m_new)
    p = jnp.exp(s - m_new)
    return (m_new, a * l + jnp.sum(p, axis=-1, keepdims=True), a * acc + _dot(p.astype(BF16), v))


def _attn_a_prompt_kernel(q_ref, k_ref, v_ref, bnear_ref, bfar_ref, dl_ref, g_ref, o_ref, kbf, vbf,
                          *, bq, scale, lam_init):
    qb = pl.program_id(2)

    @pl.when(qb == 0)
    def _():
        _cast_rows(kbf, k_ref)
        _cast_rows(vbf, v_ref)

    q1, q2 = _split_q(q_ref[...], HEAD_LANES // 2)
    bfar = bfar_ref[0]

    def block(koff, bias, st):
        k = kbf[pl.ds(koff, bq), :]
        v = vbf[pl.ds(koff, bq), :]
        st1, st2 = st
        return (_flash_step(_dot_nt(q1, k) * scale + bias, v, st1),
                _flash_step(_dot_nt(q2, k) * scale + bias, v, st2))

    init = (jnp.full((bq, 1), NEG, F32), jnp.zeros((bq, 1), F32), jnp.zeros((bq, HEAD_LANES), F32))
    st = lax.fori_loop(0, jnp.maximum(qb - 1, 0),
                       lambda kb, st: block(pl.multiple_of(kb * bq, bq), bfar, st), (init, init))
    prev_pen = jnp.where(qb > 0, 0.0, NEG)
    st = block(pl.multiple_of(jnp.maximum(qb - 1, 0) * bq, bq), bnear_ref[0, :, 0:bq] + prev_pen, st)
    st = block(pl.multiple_of(qb * bq, bq), bnear_ref[0, :, bq:2 * bq], st)
    (_, l1, a1), (_, l2, a2) = st
    lam = _diff_lambda(dl_ref, lam_init)
    o = a1 / l1 - lam * (a2 / l2)
    o_ref[...] = (_rms(o, g_ref[...]) * (1.0 - lam_init)).astype(o_ref.dtype)


def attn_a_prompt(cfg, z, bnear, bfar, dl, g, lam_init):
    B, S, H, bq = cfg.batch, cfg.seq, cfg.a_heads, cfg.bq_a
    nq = S // bq
    return pl.pallas_call(
        functools.partial(_attn_a_prompt_kernel, bq=bq, scale=cfg.a_qk ** -0.5, lam_init=lam_init),
        out_shape=jax.ShapeDtypeStruct((B * S, cfg.a_width), BF16),
        grid=(B, H, nq),
        in_specs=[pl.BlockSpec((bq, HEAD_LANES), lambda b, h, i: (b * nq + i, h)),
                  pl.BlockSpec((S, HEAD_LANES), lambda b, h, i: (b, H + h)),
                  pl.BlockSpec((S, HEAD_LANES), lambda b, h, i: (b, 2 * H + h)),
                  pl.BlockSpec((1, bq, 2 * bq), lambda b, h, i: (h, 0, 0)),
                  pl.BlockSpec((1, 1, bq), lambda b, h, i: (h, 0, 0)),
                  pl.BlockSpec(dl.shape, lambda b, h, i: (0, 0)),
                  pl.BlockSpec((1, HEAD_LANES), lambda b, h, i: (0, 0))],
        out_specs=pl.BlockSpec((bq, HEAD_LANES), lambda b, h, i: (b * nq + i, h)),
        scratch_shapes=[pltpu.VMEM((S, HEAD_LANES), BF16), pltpu.VMEM((S, HEAD_LANES), BF16)],
        compiler_params=_cparams(3, VMEM_LIMIT_V7X), name="attn_a_prompt",
    )(z, z, z, bnear, bfar, dl, g.reshape(1, HEAD_LANES))


def _softmax_parts(parts):
    m = functools.reduce(jnp.maximum, [jnp.max(s, axis=-1, keepdims=True) for s in parts])
    ps = [jnp.exp(s - m) for s in parts]
    l = functools.reduce(jnp.add, [jnp.sum(p, axis=-1, keepdims=True) for p in ps])
    return [p / l for p in ps]


def _attn_a_sample_kernel(q_ref, kn_ref, vn_ref, ck_ref, cv_ref, bias_ref, dl_ref, g_ref, o_ref,
                          *, heads, past, scale, lam_init):
    lam = _diff_lambda(dl_ref, lam_init)
    for h in range(heads):
        hs = slice(h * HEAD_LANES, (h + 1) * HEAD_LANES)
        q1, q2 = _split_q(q_ref[:, hs], HEAD_LANES // 2)
        kp = ck_ref[0, :, hs].astype(BF16)
        kn = kn_ref[:, hs].astype(BF16)
        bp = bias_ref[h, :, 0:past]
        bn = bias_ref[h, :, past:]
        p1 = _softmax_parts([_dot_nt(q1, kp) * scale + bp, _dot_nt(q1, kn) * scale + bn])
        p2 = _softmax_parts([_dot_nt(q2, kp) * scale + bp, _dot_nt(q2, kn) * scale + bn])
        o = (_dot((p1[0] - lam * p2[0]).astype(BF16), cv_ref[0, :, hs].astype(BF16))
             + _dot((p1[1] - lam * p2[1]).astype(BF16), vn_ref[:, hs].astype(BF16)))
        o_ref[:, hs] = (_rms(o, g_ref[...]) * (1.0 - lam_init)).astype(o_ref.dtype)


def attn_a_sample(cfg, z, cache_k, cache_v, bias, dl, g, lam_init):
    Bd, T, H, P = cfg.dec_batch, cfg.dec_seq, cfg.a_heads, cfg.past_len
    W = cfg.a_width
    r0 = cfg.rows_p // T
    new = lambda c: pl.BlockSpec((T, W), lambda b: (r0 + b, c))
    cache = pl.BlockSpec((1, P, W), lambda b: (b, 0, 0))
    return pl.pallas_call(
        functools.partial(_attn_a_sample_kernel, heads=H, past=P, scale=cfg.a_qk ** -0.5, lam_init=lam_init),
        out_shape=jax.ShapeDtypeStruct((Bd * T, W), BF16),
        grid=(Bd,),
        in_specs=[new(0), new(1), new(2), cache, cache,
                  pl.BlockSpec(bias.shape, lambda b: (0, 0, 0)),
                  pl.BlockSpec(dl.shape, lambda b: (0, 0)),
                  pl.BlockSpec((1, HEAD_LANES), lambda b: (0, 0))],
        out_specs=pl.BlockSpec((T, W), lambda b: (b, 0)),
        compiler_params=_cparams(1, VMEM_LIMIT_V7X), name="attn_a_sample",
    )(z, z, z, cache_k.reshape(Bd, P, W), cache_v.reshape(Bd, P, W), bias, dl, g.reshape(1, HEAD_LANES))


def _attn_b_prompt_kernel(q_ref, k0, k1, k2, v0, v1, v2, bias_ref, o_ref, *, bq, scale):
    i = pl.program_id(2)
    q = q_ref[...].astype(BF16)
    parts = []
    for j, kr in enumerate((k0, k1, k2)):
        pen = jnp.where(i - 2 + j >= 0, 0.0, NEG)
        parts.append(_dot_nt(q, kr[...].astype(BF16)) * scale + bias_ref[0, :, j * bq:(j + 1) * bq] + pen)
    ps = _softmax_parts(parts)
    o = functools.reduce(jnp.add, [_dot(p.astype(BF16), vr[...].astype(BF16)) for p, vr in zip(ps, (v0, v1, v2))])
    o_ref[...] = o.astype(o_ref.dtype)


def attn_b_prompt(cfg, z, bias):
    B, S, H, bq = cfg.batch, cfg.seq, cfg.b_heads, cfg.bq_b
    nq = S // bq
    c0 = 3 * cfg.a_width // HEAD_LANES

    def kv(col, j):
        return pl.BlockSpec((bq, HEAD_LANES),
                            lambda b, h, i: (b * nq + jnp.maximum(i - 2 + j, 0), c0 + col * H + h))

    return pl.pallas_call(
        functools.partial(_attn_b_prompt_kernel, bq=bq, scale=HEAD_LANES ** -0.5),
        out_shape=jax.ShapeDtypeStruct((B * S, cfg.b_width), BF16),
        grid=(B, H, nq),
        in_specs=[pl.BlockSpec((bq, HEAD_LANES), lambda b, h, i: (b * nq + i, c0 + h)),
                  kv(1, 0), kv(1, 1), kv(1, 2), kv(2, 0), kv(2, 1), kv(2, 2),
                  pl.BlockSpec((1, bq, 3 * bq), lambda b, h, i: (h, 0, 0))],
        out_specs=pl.BlockSpec((bq, HEAD_LANES), lambda b, h, i: (b * nq + i, h)),
        compiler_params=_cparams(3, VMEM_LIMIT_V7X), name="attn_b_prompt",
    )(z, z, z, z, z, z, z, bias)


def _attn_b_sample_kernel(q_ref, kn_ref, vn_ref, ck_ref, cv_ref, bias_ref, o_ref, *, heads, past, scale):
    for h in range(heads):
        hs = slice(h * HEAD_LANES, (h + 1) * HEAD_LANES)
        q = q_ref[:, hs].astype(BF16)
        ps = _softmax_parts([_dot_nt(q, ck_ref[0, :, hs].astype(BF16)) * scale + bias_ref[h, :, 0:past],
                             _dot_nt(q, kn_ref[:, hs].astype(BF16)) * scale + bias_ref[h, :, past:]])
        o = (_dot(ps[0].astype(BF16), cv_ref[0, :, hs].astype(BF16))
             + _dot(ps[1].astype(BF16), vn_ref[:, hs].astype(BF16)))
        o_ref[:, hs] = o.astype(o_ref.dtype)


def attn_b_sample(cfg, z, cache_k, cache_v, bias):
    Bd, T, H = cfg.dec_batch, cfg.dec_seq, cfg.b_heads
    Pc = cache_k.shape[1]
    W = cfg.b_width
    r0 = cfg.rows_p // T
    c0 = 3 * cfg.a_width // W
    new = lambda c: pl.BlockSpec((T, W), lambda b: (r0 + b, c0 + c))
    cache = pl.BlockSpec((1, Pc, W), lambda b: (b, 0, 0))
    return pl.pallas_call(
        functools.partial(_attn_b_sample_kernel, heads=H, past=Pc, scale=HEAD_LANES ** -0.5),
        out_shape=jax.ShapeDtypeStruct((Bd * T, W), BF16),
        grid=(Bd,),
        in_specs=[new(0), new(1), new(2), cache, cache, pl.BlockSpec(bias.shape, lambda b: (0, 0, 0))],
        out_specs=pl.BlockSpec((T, W), lambda b: (b, 0)),
        compiler_params=_cparams(1, VMEM_LIMIT_V7X), name="attn_b_sample",
    )(z, z, z, cache_k.reshape(Bd, Pc, W), cache_v.reshape(Bd, Pc, W), bias)


def _split3(x):
    hi = x.astype(BF16)
    r = x - hi.astype(F32)
    mid = r.astype(BF16)
    lo = (r - mid.astype(F32)).astype(BF16)
    return hi, mid, lo


def _hgrn_head(qz, fz, vv, lb, st, L):
    sig = jax.nn.sigmoid(fz)
    logf = jnp.log(lb + (1.0 - lb) * sig)
    kk = (1.0 - lb) * (1.0 - sig)
    qq = _silu(qz)
    row = lax.broadcasted_iota(jnp.int32, (L, 1), 0)
    col = lax.broadcasted_iota(jnp.int32, (1, L), 1)
    tril = (col <= row).astype(BF16)
    b = functools.reduce(jnp.add, [_dot(tril, part) for part in _split3(logf)])
    vb = vv.astype(BF16)

    o = _dot_nt((qq * jnp.exp(b)).astype(BF16), st.astype(BF16))
    b_last = b[L - 1:L, :]
    kd = kk * jnp.exp(b_last - b)
    st_new = jnp.exp(b_last) * st + _dot_tn(vb, kd.astype(BF16))

    if L > SUB:
        p_off = jnp.zeros((L, L), F32)
        m = SUB
        while m < L:
            is_q = (row // m) % 2 == 1
            ref = b[m - 1:m, :]
            for g in range(1, L // (2 * m)):
                ref = jnp.where(row >= g * 2 * m, b[g * 2 * m + m - 1:g * 2 * m + m, :], ref)
            d = b - ref
            x = (jnp.where(is_q, qq, kk) * jnp.exp(jnp.where(is_q, d, -d))).astype(BF16)
            pair = jnp.logical_and(is_q, col // m == row // m - 1)
            p_off = p_off + jnp.where(pair, _dot_nt(x, x), 0.0)
            m *= 2
        o = o + _dot(p_off.astype(BF16), vb)

    ones = jnp.ones((HEAD_LANES, HEAD_LANES), BF16)
    t_idx = lax.broadcasted_iota(jnp.int32, (SUB, 1), 0)
    diag = []
    for r0 in range(0, L, SUB):
        bj, qj, kj, vj = (a[r0:r0 + SUB, :] for a in (b, qq, kk, vv))
        w = [jnp.exp(jnp.where(t_idx >= s, bj - bj[s:s + 1, :], NEG)) * qj * kj[s:s + 1, :] for s in range(SUB)]
        r = _dot(jnp.concatenate(w, axis=0).astype(BF16), ones)
        diag.append(functools.reduce(jnp.add, [r[s * SUB:(s + 1) * SUB, :] * vj[s:s + 1, :] for s in range(SUB)]))
    o = o + (jnp.concatenate(diag, axis=0) if len(diag) > 1 else diag[0])
    return o, st_new


def _hgrn_kernel(q_ref, f_ref, i_ref, g_ref, lb_ref, ng_ref, s0_ref, o_ref, so_ref, st_ref, *, heads, L):
    c = pl.program_id(1)

    @pl.when(c == 0)
    def _():
        st_ref[...] = s0_ref[0]

    for h in range(heads):
        hs = slice(h * HEAD_LANES, (h + 1) * HEAD_LANES)
        o, st_new = _hgrn_head(q_ref[:, hs], f_ref[:, hs], i_ref[:, hs], lb_ref[:, hs], st_ref[h], L)
        st_ref[h] = st_new
        o_ref[:, hs] = (_rms(o, ng_ref[...]) * _silu(g_ref[:, hs])).astype(o_ref.dtype)

    @pl.when(c == pl.num_programs(1) - 1)
    def _():
        so_ref[0] = st_ref[...]


def hgrn2(cfg, z, lb, norm_g, s0t, *, nb, T, row0):
    H, W = cfg.c_heads, cfg.c_width
    L = min(cfg.chunk, T)
    nc = T // L
    c0 = (3 * cfg.a_width + 3 * cfg.b_width) // W
    r0 = row0 // L
    col = lambda c: pl.BlockSpec((L, W), lambda b, i: (r0 + b * nc + i, c0 + c))
    state = pl.BlockSpec((1, H, HEAD_LANES, HEAD_LANES), lambda b, i: (b, 0, 0, 0))
    return pl.pallas_call(
        functools.partial(_hgrn_kernel, heads=H, L=L),
        out_shape=[jax.ShapeDtypeStruct((nb * T, W), BF16),
                   jax.ShapeDtypeStruct((nb, H, HEAD_LANES, HEAD_LANES), F32)],
        grid=(nb, nc),
        in_specs=[col(0), col(1), col(2), col(3),
                  pl.BlockSpec((1, W), lambda b, i: (0, 0)),
                  pl.BlockSpec((1, HEAD_LANES), lambda b, i: (0, 0)),
                  state],
        out_specs=[pl.BlockSpec((L, W), lambda b, i: (b * nc + i, 0)), state],
        scratch_shapes=[pltpu.VMEM((H, HEAD_LANES, HEAD_LANES), F32)],
        compiler_params=_cparams(2, VMEM_LIMIT_V7X), name="hgrn2",
    )(z, z, z, z, lb.reshape(1, W), norm_g.reshape(1, HEAD_LANES), s0t)


def _t5_bucket(cfg, rel):
    half = cfg.t5_buckets // 2
    n = -rel
    ret = jnp.where(n < 0, half, 0)
    n = jnp.abs(n)
    max_exact = half // 2
    nf = jnp.maximum(n, 1).astype(F32)
    large = max_exact + (jnp.log(nf / max_exact) / math.log(cfg.t5_max_dist / max_exact)
                         * (half - max_exact)).astype(jnp.int32)
    large = jnp.minimum(large, half - 1)
    return ret + jnp.where(n < max_exact, n, large)


def _t5_bias(cfg, table, qpos, kpos):
    return jnp.moveaxis(table[_t5_bucket(cfg, kpos[None, :] - qpos[:, None])], -1, 0)


def _bias_tables(cfg, t5_table):
    bq = cfg.bq_a
    qpos = jnp.arange(bq)
    kpos = jnp.arange(-bq, bq)
    near = _t5_bias(cfg, t5_table, qpos, kpos)
    visible = (kpos[None, :] // cfg.chunk) <= (qpos[:, None] // cfg.chunk)
    near = jnp.where(visible[None], near, NEG)
    far = jnp.broadcast_to(_t5_bias(cfg, t5_table, jnp.array([2 * bq]), jnp.array([0])), (cfg.a_heads, 1, bq))
    sq = cfg.past_len + jnp.arange(cfg.dec_seq)
    sample = _t5_bias(cfg, t5_table, sq, jnp.arange(cfg.past_len + cfg.dec_seq))
    return near.astype(F32), far.astype(F32), sample.astype(F32)


def _rel_bias(cfg, table, dist):
    return table[:, jnp.clip(dist, -cfg.b_max_dist, cfg.b_max_dist) + cfg.b_max_dist]


def _band_tables(cfg, rel_table, n_cache):
    bq = cfg.bq_b
    qi = jnp.arange(bq)
    kj = jnp.arange(-2 * bq, bq)
    prompt = _rel_bias(cfg, rel_table, qi[:, None] - kj[None, :])
    qc, kc = qi[:, None] // cfg.chunk, kj[None, :] // cfg.chunk
    band = (kc <= qc) & (kc >= qc - cfg.b_left_chunks)
    prompt = jnp.where(band[None], prompt, NEG)
    sq = cfg.past_len + jnp.arange(cfg.dec_seq)
    kbpos = jnp.concatenate([cfg.past_len - n_cache + jnp.arange(n_cache), sq])
    sample = _rel_bias(cfg, rel_table, sq[:, None] - kbpos[None, :])
    return prompt.astype(F32), sample.astype(F32)


def _check(cfg):
    assert cfg.rows_s % 16 == 0 and cfg.seq % cfg.rows_s == 0
    assert cfg.rows % cfg.bm == 0 and cfg.bm % 16 == 0
    assert cfg.d_model % cfg.bn == 0 and (cfg.n_in - 3 * cfg.d_model) % cfg.bn == 0 and cfg.n_in % cfg.bn == 0
    assert (3 * cfg.a_width + 3 * cfg.b_width) % cfg.c_width == 0 and (3 * cfg.a_width) % cfg.b_width == 0
    assert cfg.seq % cfg.bq_a == 0 and cfg.bq_a % cfg.chunk == 0 and cfg.bq_a >= cfg.t5_max_dist
    assert cfg.seq % cfg.bq_b == 0 and cfg.bq_b % cfg.chunk == 0 and 2 * cfg.bq_b >= cfg.b_win
    assert cfg.seq % cfg.chunk == 0 and cfg.chunk % SUB == 0 and cfg.dec_seq % SUB == 0
    assert cfg.past_len % cfg.chunk == 0 and cfg.dec_seq <= cfg.chunk
    assert min(cfg.b_win, cfg.past_len) <= cfg.b_win and cfg.rows_p % cfg.dec_seq == 0


def step(cfg, x_prompt, x_sample, cache_a_k, cache_a_v, cache_b_k, cache_b_v, state_c, c_prompt, c_sample,
         w_ada, b_ada, w_in, diff_lambda, a_subln_g, t5_bias, b_rel_bias, c_lb_param, c_norm_g,
         w_branch_a, w_branch_b, w_branch_c, w_o, ln1_g, ln1_b, ln2_g, ln2_b, w_up, w_down):
    _check(cfg)
    D, B, S, Bd, T = cfg.d_model, cfg.batch, cfg.seq, cfg.dec_batch, cfg.dec_seq
    RP = cfg.rows_p
    n_ctx = B + Bd
    pad = (-n_ctx) % 8
    c_all = jnp.concatenate([c_prompt, c_sample, jnp.zeros((pad, D), F32)], axis=0)
    mod = adaln_all(cfg, c_all, w_ada, b_ada)

    lb_soft = jax.nn.softmax(c_lb_param.astype(F32), axis=0)
    lb_all = jnp.cumsum(lb_soft, axis=0) - lb_soft[0]
    a_near, a_far, a_sample = _bias_tables(cfg, t5_bias)
    n_cache = cache_b_k.shape[2]
    zero_state = jnp.zeros((B, cfg.c_heads, HEAD_LANES, HEAD_LANES), F32)

    x = jnp.concatenate([x_prompt.reshape(RP, D), x_sample.reshape(Bd * T, D)], axis=0)
    (h,) = rowwise(cfg, x, mod_next=mod[0], sc_col=1, sh_col=0)
    outs = [[] for _ in range(10)]
    A, Bw, C = cfg.a_width, cfg.b_width, cfg.c_width
    n_b_rows = min(cfg.b_win, S)
    for l in range(cfg.depth):
        lam_init = 0.8 - 0.6 * math.exp(-0.3 * l)
        z = matmul(cfg, h, w_in[l], name="w_in")
        b_prompt, b_sample = _band_tables(cfg, b_rel_bias[l], n_cache)

        oa = jnp.concatenate([attn_a_prompt(cfg, z, a_near, a_far, diff_lambda[l], a_subln_g[l], lam_init),
                              attn_a_sample(cfg, z, cache_a_k[l], cache_a_v[l], a_sample, diff_lambda[l],
                                            a_subln_g[l], lam_init)], axis=0)
        ob = jnp.concatenate([attn_b_prompt(cfg, z, b_prompt),
                              attn_b_sample(cfg, z, cache_b_k[l], cache_b_v[l], b_sample)], axis=0)
        oc_p, st_p = hgrn2(cfg, z, lb_all[l], c_norm_g[l], zero_state, nb=B, T=S, row0=0)
        oc_s, st_s = hgrn2(cfg, z, lb_all[l], c_norm_g[l], jnp.swapaxes(state_c[l], -1, -2), nb=Bd, T=T, row0=RP)
        oc = jnp.concatenate([oc_p, oc_s], axis=0)

        mixed = branch_mix(cfg, oa, ob, oc, z, w_branch_a[l], w_branch_b[l], w_branch_c[l])
        y = matmul(cfg, mixed, w_o[l], name="w_o")
        x, h2 = rowwise(cfg, x, y, mod[l], ln1_g[l], ln1_b[l], mod[l], g_col=2, sc_col=4, sh_col=3)
        u = matmul(cfg, h2, w_up[l], out_dtype=BF16, relu2=True, name="w_up")
        m = matmul(cfg, u, w_down[l], bk=D, name="w_down")
        if l + 1 < cfg.depth:
            x, h = rowwise(cfg, x, m, mod[l], ln2_g[l], ln2_b[l], mod[l + 1], g_col=5, sc_col=1, sh_col=0)
        else:
            (x,) = rowwise(cfg, x, m, mod[l], ln2_g[l], ln2_b[l], g_col=5)

        zp = z[:RP].reshape(B, S, -1)
        zs = z[RP:].reshape(Bd, T, -1)
        outs[0].append(zp[..., A:2 * A].reshape(B, S, cfg.a_heads, HEAD_LANES))
        outs[1].append(zp[..., 2 * A:3 * A].reshape(B, S, cfg.a_heads, HEAD_LANES))
        outs[2].append(zp[:, S - n_b_rows:, 3 * A + Bw:3 * A + 2 * Bw].reshape(B, n_b_rows, cfg.b_heads, HEAD_LANES))
        outs[3].append(zp[:, S - n_b_rows:, 3 * A + 2 * Bw:3 * A + 3 * Bw].reshape(B, n_b_rows, cfg.b_heads, HEAD_LANES))
        outs[4].append(jnp.swapaxes(st_p, -1, -2))
        outs[5].append(zs[..., A:2 * A].reshape(Bd, T, cfg.a_heads, HEAD_LANES))
        outs[6].append(zs[..., 2 * A:3 * A].reshape(Bd, T, cfg.a_heads, HEAD_LANES))
        outs[7].append(zs[..., 3 * A + Bw:3 * A + 2 * Bw].reshape(Bd, T, cfg.b_heads, HEAD_LANES))
        outs[8].append(zs[..., 3 * A + 2 * Bw:3 * A + 3 * Bw].reshape(Bd, T, cfg.b_heads, HEAD_LANES))
        outs[9].append(jnp.swapaxes(st_s, -1, -2))

    return (x[:RP].reshape(B, S, D), x[RP:].reshape(Bd, T, D)) + tuple(jnp.stack(o) for o in outs)


def kernel(x_prompt, x_sample, cache_a_k, cache_a_v, cache_b_k, cache_b_v, state_c, c_prompt, c_sample, w_ada, b_ada, w_in, diff_lambda, a_subln_g, t5_bias, b_rel_bias, c_lb_param, c_norm_g, w_branch_a, w_branch_b, w_branch_c, w_o, ln1_g, ln1_b, ln2_g, ln2_b, w_up, w_down):
    return step(Cfg(), x_prompt, x_sample, cache_a_k, cache_a_v, cache_b_k, cache_b_v, state_c, c_prompt, c_sample,
                w_ada, b_ada, w_in, diff_lambda, a_subln_g, t5_bias, b_rel_bias, c_lb_param, c_norm_g,
                w_branch_a, w_branch_b, w_branch_c, w_o, ln1_g, ln1_b, ln2_g, ln2_b, w_up, w_down)
```

```python
import functools
import math
from typing import NamedTuple

import jax
import jax.numpy as jnp
import numpy as np
from jax import lax
from jax.experimental import pallas as pl
from jax.experimental.pallas import tpu as pltpu

F32 = jnp.float32
BF16 = jnp.bfloat16
NEG = -1e30
LOG2E = math.log2(math.e)
LN_EPS = 1e-5
RMS_EPS = 1e-6
HEAD_LANES = 128
SUB = 16
VMEM_LIMIT_V7X = 56 * 1024 * 1024


class Cfg(NamedTuple):
    d_model: int = 4096
    batch: int = 2
    seq: int = 4096
    depth: int = 4
    dec_batch: int = 8
    dec_seq: int = 16
    past_len: int = 1024
    chunk: int = 64
    a_heads: int = 12
    a_qk: int = 64
    b_heads: int = 12
    b_left_chunks: int = 8
    b_max_dist: int = 128
    c_heads: int = 8
    t5_buckets: int = 32
    t5_max_dist: int = 128
    bm: int = 640
    bn: int = 512
    bn_wide: int = 1024
    bq_a: int = 512
    bq_b: int = 256

    @property
    def a_width(self): return self.a_heads * HEAD_LANES
    @property
    def b_width(self): return self.b_heads * HEAD_LANES
    @property
    def c_width(self): return self.c_heads * HEAD_LANES
    @property
    def d_ff(self): return 4 * self.d_model
    @property
    def b_win(self): return self.b_left_chunks * self.chunk
    @property
    def n_in(self): return 3 * self.a_width + 3 * self.b_width + 4 * self.c_width + 3 * self.d_model
    @property
    def rows_p(self): return self.batch * self.seq
    @property
    def rows_s(self): return self.dec_batch * self.dec_seq
    @property
    def rows(self): return self.rows_p + self.rows_s
    @property
    def alpha(self): return (2 * self.depth) ** 0.25


def _cparams(n_axes, vmem=None):
    return pltpu.CompilerParams(dimension_semantics=("arbitrary",) * n_axes, vmem_limit_bytes=vmem)


def _dot(a, b):
    return jnp.dot(a, b, preferred_element_type=F32)


def _dot_nt(a, b):
    return lax.dot_general(a, b, (((1,), (1,)), ((), ())), preferred_element_type=F32)


def _dot_tn(a, b):
    return lax.dot_general(a, b, (((0,), (0,)), ((), ())), preferred_element_type=F32)


def _silu(x):
    return x * jax.nn.sigmoid(x)


def _cast_rows(dst_ref, src_ref, rows_per_step=512):
    n = src_ref.shape[0]
    step = min(rows_per_step, n)
    for r in range(0, n, step):
        dst_ref[r:r + step, :] = src_ref[r:r + step, :].astype(BF16)


def _adaln_kernel(c_ref, w_ref, b_ref, o_ref):
    s = _silu(c_ref[...]).astype(BF16)
    o_ref[0] = _dot(s, w_ref[0].astype(BF16)) + b_ref[0]


def adaln_all(cfg, c_all, w_ada, b_ada):
    R, D = c_all.shape
    L, _, N = w_ada.shape
    bn = cfg.bn
    return pl.pallas_call(
        _adaln_kernel,
        out_shape=jax.ShapeDtypeStruct((L, R, N), F32),
        grid=(L, N // bn),
        in_specs=[pl.BlockSpec((R, D), lambda l, j: (0, 0)),
                  pl.BlockSpec((1, D, bn), lambda l, j: (l, 0, j)),
                  pl.BlockSpec((1, 1, bn), lambda l, j: (l, 0, j))],
        out_specs=pl.BlockSpec((1, R, bn), lambda l, j: (l, 0, j)),
        compiler_params=_cparams(2, VMEM_LIMIT_V7X),
        name="adaln",
    )(c_all, w_ada, b_ada.reshape(L, 1, N))


def _rowwise_kernel(*refs, cfg, n_x, n_xo, do_ln, do_mod, g_col, sc_col, sh_col):
    refs = list(refs)
    x_refs = [refs.pop(0) for _ in range(n_x)]
    y_ref = refs.pop(0) if do_ln else None
    modg_ref = refs.pop(0) if do_ln else None
    lng_ref = refs.pop(0) if do_ln else None
    lnb_ref = refs.pop(0) if do_ln else None
    modn_ref = refs.pop(0) if do_mod else None
    xo_refs = [refs.pop(0) for _ in range(n_xo)]
    ho_ref = refs.pop(0) if do_mod else None
    D = cfg.d_model
    bm = cfg.rows_s
    n_prompt_blocks = cfg.rows_p // bm
    blocks_per_batch = cfg.seq // bm
    i = pl.program_id(0)

    def slab(x_ref, xo_ref, r0, nrows, brow):
        x = x_ref[r0:r0 + nrows, :]
        if do_ln:
            g = modg_ref[pl.ds(brow, 1), g_col * D:(g_col + 1) * D]
            u = cfg.alpha * x + (1.0 + g) * y_ref[r0:r0 + nrows, :]
            mu = jnp.mean(u, axis=-1, keepdims=True)
            var = jnp.mean(jnp.square(u - mu), axis=-1, keepdims=True)
            x = (u - mu) * lax.rsqrt(var + LN_EPS) * lng_ref[...] + lnb_ref[...]
        if xo_ref is not None:
            xo_ref[r0:r0 + nrows, :] = x
        if do_mod:
            sc = modn_ref[pl.ds(brow, 1), sc_col * D:(sc_col + 1) * D]
            sh = modn_ref[pl.ds(brow, 1), sh_col * D:(sh_col + 1) * D]
            ho_ref[r0:r0 + nrows, :] = (x * (1.0 + sc) + sh).astype(BF16)

    @pl.when(i < n_prompt_blocks)
    def _():
        slab(x_refs[0], xo_refs[0] if xo_refs else None, 0, bm, i // blocks_per_batch)

    @pl.when(i >= n_prompt_blocks)
    def _():
        for s in range(cfg.dec_batch):
            slab(x_refs[-1], xo_refs[-1] if xo_refs else None, s * cfg.dec_seq, cfg.dec_seq, cfg.batch + s)


def rowwise(cfg, x, y=None, mod_gate=None, ln_g=None, ln_b=None, mod_next=None, *, split_out=False,
            g_col=0, sc_col=0, sh_col=0):
    do_ln = y is not None
    do_mod = mod_next is not None
    xs = list(x) if isinstance(x, (tuple, list)) else [x]
    D = cfg.d_model
    M, bm = cfg.rows, cfg.rows_s
    n_prompt_blocks = cfg.rows_p // bm
    row = pl.BlockSpec((bm, D), lambda i: (i, 0))
    row_p = pl.BlockSpec((bm, D), lambda i: (jnp.minimum(i, n_prompt_blocks - 1), 0))
    row_s = pl.BlockSpec((bm, D), lambda i: (0, 0))
    full = lambda a: pl.BlockSpec(a.shape, lambda i: (0,) * a.ndim)
    args = list(xs)
    specs = [row] if len(xs) == 1 else [row_p, row_s]
    outs, out_specs = [], []
    if do_ln:
        ln_g = ln_g.reshape(1, D)
        ln_b = ln_b.reshape(1, D)
        args += [y, mod_gate, ln_g, ln_b]
        specs += [row, full(mod_gate), full(ln_g), full(ln_b)]
    if do_mod:
        args.append(mod_next)
        specs.append(full(mod_next))
    n_xo = 0
    if do_ln or len(xs) == 2:
        if split_out:
            outs += [jax.ShapeDtypeStruct((cfg.rows_p, D), F32), jax.ShapeDtypeStruct((cfg.rows_s, D), F32)]
            out_specs += [row_p, row_s]
        else:
            outs.append(jax.ShapeDtypeStruct((M, D), F32))
            out_specs.append(row)
        n_xo = len(outs)
    if do_mod:
        outs.append(jax.ShapeDtypeStruct((M, D), BF16))
        out_specs.append(row)
    return pl.pallas_call(
        functools.partial(_rowwise_kernel, cfg=cfg, n_x=len(xs), n_xo=n_xo, do_ln=do_ln, do_mod=do_mod,
                          g_col=g_col, sc_col=sc_col, sh_col=sh_col),
        out_shape=outs, grid=(M // bm,), in_specs=specs, out_specs=out_specs,
        compiler_params=_cparams(1, VMEM_LIMIT_V7X), name="rowwise",
    )(*args)


def _mm_kernel(x_ref, w_ref, o_ref, wbf_ref, *acc, nk, nch, ck, n_wsteps, relu2):
    t = pl.program_id(0)
    i = pl.program_id(1)

    @pl.when(jnp.logical_and(t < n_wsteps, i < nch))
    def _():
        wbf_ref[t % 2, pl.ds(pl.multiple_of(i * ck, ck), ck), :] = w_ref[...].astype(BF16)

    @pl.when(t >= 1)
    def _():
        s = t - 1
        d = _dot(x_ref[...], wbf_ref[s % 2])

        def finish(r):
            if relu2:
                r = jnp.square(jnp.maximum(r, 0.0))
            o_ref[...] = r.astype(o_ref.dtype)

        if nk == 1:
            finish(d)
        else:
            acc_ref = acc[0]
            k = s % nk

            @pl.when(k == 0)
            def _():
                acc_ref[i] = d

            @pl.when(jnp.logical_and(k > 0, k < nk - 1))
            def _():
                acc_ref[i] += d

            @pl.when(k == nk - 1)
            def _():
                finish(acc_ref[i] + d)


def matmul(cfg, x, w, layer, *, bn, out_dtype=F32, relu2=False, bk=None, name="matmul"):
    M, K = x.shape
    N = w.shape[-1]
    bm = cfg.bm
    bk = K if bk is None else bk
    nk, nm = K // bk, M // bm
    nch = max(c for c in (1, 2, 4, 8) if c <= nm and bk % (16 * c) == 0)
    ck = bk // nch
    S = (N // bn) * nk
    assert N % bn == 0 and M % bm == 0 and K % bk == 0

    def x_map(t, i):
        return (jnp.where(t == 0, 0, i), jnp.maximum(t - 1, 0) % nk)

    def w_map(t, i):
        tq = jnp.minimum(t, S - 1)
        c = jnp.where(t < S, jnp.minimum(i, nch - 1), nch - 1)
        return (layer, (tq % nk) * nch + c, tq // nk)

    def o_map(t, i):
        s = jnp.maximum(t - 1, 0)
        writes = jnp.logical_and(t >= 1, s % nk == nk - 1)
        return (jnp.where(writes, i, 0), s // nk)

    scratch = [pltpu.VMEM((2, bk, bn), BF16)]
    if nk > 1:
        scratch.append(pltpu.VMEM((nm, bm, bn), F32))
    return pl.pallas_call(
        functools.partial(_mm_kernel, nk=nk, nch=nch, ck=ck, n_wsteps=S, relu2=relu2),
        out_shape=jax.ShapeDtypeStruct((M, N), out_dtype),
        grid=(S + 1, nm),
        in_specs=[pl.BlockSpec((bm, bk), x_map), pl.BlockSpec((None, ck, bn), w_map)],
        out_specs=pl.BlockSpec((bm, bn), o_map),
        scratch_shapes=scratch,
        compiler_params=_cparams(2, VMEM_LIMIT_V7X), name=name,
    )(x, w)


def _mix_kernel(oa_ref, ob_ref, oc_ref, ga_ref, gb_ref, gc_ref, wa_ref, wb_ref, wc_ref, o_ref, wa_s, wb_s, wc_s):
    @pl.when(pl.program_id(1) == 0)
    def _():
        _cast_rows(wa_s, wa_ref)
        _cast_rows(wb_s, wb_ref)
        _cast_rows(wc_s, wc_ref)

    mixed = (jax.nn.sigmoid(ga_ref[...]) * _dot(oa_ref[...], wa_s[...])
             + jax.nn.sigmoid(gb_ref[...]) * _dot(ob_ref[...], wb_s[...])
             + jax.nn.sigmoid(gc_ref[...]) * _dot(oc_ref[...], wc_s[...]))
    o_ref[...] = mixed.astype(o_ref.dtype)


def branch_mix(cfg, oa, ob, oc, z, wa, wb, wc, layer):
    M = oa.shape[0]
    D = cfg.d_model
    bm, bn = cfg.bm, cfg.bn
    g0 = (3 * cfg.a_width + 3 * cfg.b_width + 4 * cfg.c_width) // bn
    gstep = D // bn
    act = lambda w: pl.BlockSpec((bm, w), lambda j, i: (i, 0))
    gate = lambda t: pl.BlockSpec((bm, bn), lambda j, i: (i, g0 + t * gstep + j))
    wgt = lambda w: pl.BlockSpec((None, w, bn), lambda j, i: (layer, 0, j))
    return pl.pallas_call(
        _mix_kernel,
        out_shape=jax.ShapeDtypeStruct((M, D), BF16),
        grid=(D // bn, M // bm),
        in_specs=[act(cfg.a_width), act(cfg.b_width), act(cfg.c_width), gate(0), gate(1), gate(2),
                  wgt(cfg.a_width), wgt(cfg.b_width), wgt(cfg.c_width)],
        out_specs=pl.BlockSpec((bm, bn), lambda j, i: (i, j)),
        scratch_shapes=[pltpu.VMEM((cfg.a_width, bn), BF16), pltpu.VMEM((cfg.b_width, bn), BF16),
                        pltpu.VMEM((cfg.c_width, bn), BF16)],
        compiler_params=_cparams(2, VMEM_LIMIT_V7X), name="branch_mix",
    )(oa, ob, oc, z, z, z, wa, wb, wc)


def _diff_lambda(dl_ref, lam_init):
    dl = dl_ref[...]
    s1 = jnp.sum(dl[0:1] * dl[1:2], axis=-1, keepdims=True)
    s2 = jnp.sum(dl[2:3] * dl[3:4], axis=-1, keepdims=True)
    return jnp.exp(s1) - jnp.exp(s2) + lam_init


def _split_q(q, half):
    lane = lax.broadcasted_iota(jnp.int32, q.shape, 1)
    return (jnp.where(lane < half, q, 0.0).astype(BF16), jnp.where(lane >= half, q, 0.0).astype(BF16))


def _rms(o, g):
    return o * lax.rsqrt(jnp.mean(jnp.square(o), axis=-1, keepdims=True) + RMS_EPS) * g


def _flash_step(s, v, st):
    m, l, acc = st
    m_new = jnp.maximum(m, jnp.max(s, axis=-1, keepdims=True))
    a = jnp.exp2(m - m_new)
    p = jnp.exp2(s - m_new)
    return (m_new, a * l + jnp.sum(p, axis=-1, keepdims=True), a * acc + _dot(p.astype(BF16), v))


def _attn_a_prompt_kernel(q_ref, k_ref, v_ref, bnear_ref, dl_ref, g_ref, o_ref, kbf, vbf,
                          *, bq, scale, lam_init):
    qb = pl.program_id(2)

    @pl.when(qb == 0)
    def _():
        _cast_rows(kbf, k_ref)
        _cast_rows(vbf, v_ref)

    q1, q2 = _split_q(q_ref[...] * (scale * LOG2E), HEAD_LANES // 2)

    def block(koff, bias, st):
        k = kbf[pl.ds(koff, bq), :]
        v = vbf[pl.ds(koff, bq), :]
        s1, s2 = _dot_nt(q1, k), _dot_nt(q2, k)
        if bias is not None:
            s1, s2 = s1 + bias, s2 + bias
        return (_flash_step(s1, v, st[0]), _flash_step(s2, v, st[1]))

    init = (jnp.full((bq, 1), NEG, F32), jnp.zeros((bq, 1), F32), jnp.zeros((bq, HEAD_LANES), F32))
    st = lax.fori_loop(0, jnp.maximum(qb - 1, 0),
                       lambda kb, st: block(pl.multiple_of(kb * bq, bq), None, st), (init, init))
    prev_pen = jnp.where(qb > 0, 0.0, NEG)
    st = block(pl.multiple_of(jnp.maximum(qb - 1, 0) * bq, bq), bnear_ref[0, :, 0:bq] + prev_pen, st)
    st = block(pl.multiple_of(qb * bq, bq), bnear_ref[0, :, bq:2 * bq], st)
    (_, l1, a1), (_, l2, a2) = st
    lam = _diff_lambda(dl_ref, lam_init)
    o = a1 / l1 - lam * (a2 / l2)
    o_ref[...] = (_rms(o, g_ref[...]) * (1.0 - lam_init)).astype(o_ref.dtype)


def attn_a_prompt(cfg, z, bnear, dl, g, lam_init):
    B, S, H, bq = cfg.batch, cfg.seq, cfg.a_heads, cfg.bq_a
    nq = S // bq
    return pl.pallas_call(
        functools.partial(_attn_a_prompt_kernel, bq=bq, scale=cfg.a_qk ** -0.5, lam_init=lam_init),
        out_shape=jax.ShapeDtypeStruct((B * S, cfg.a_width), BF16),
        grid=(B, H, nq),
        in_specs=[pl.BlockSpec((bq, HEAD_LANES), lambda b, h, i: (b * nq + i, h)),
                  pl.BlockSpec((S, HEAD_LANES), lambda b, h, i: (b, H + h)),
                  pl.BlockSpec((S, HEAD_LANES), lambda b, h, i: (b, 2 * H + h)),
                  pl.BlockSpec((1, bq, 2 * bq), lambda b, h, i: (h, 0, 0)),
                  pl.BlockSpec(dl.shape, lambda b, h, i: (0, 0)),
                  pl.BlockSpec((1, HEAD_LANES), lambda b, h, i: (0, 0))],
        out_specs=pl.BlockSpec((bq, HEAD_LANES), lambda b, h, i: (b * nq + i, h)),
        scratch_shapes=[pltpu.VMEM((S, HEAD_LANES), BF16), pltpu.VMEM((S, HEAD_LANES), BF16)],
        compiler_params=_cparams(3, VMEM_LIMIT_V7X), name="attn_a_prompt",
    )(z, z, z, bnear, dl, g.reshape(1, HEAD_LANES))


def _softmax_parts(parts):
    m = functools.reduce(jnp.maximum, [jnp.max(s, axis=-1, keepdims=True) for s in parts])
    ps = [jnp.exp(s - m) for s in parts]
    l = functools.reduce(jnp.add, [jnp.sum(p, axis=-1, keepdims=True) for p in ps])
    return [p / l for p in ps]


def _attn_a_sample_kernel(q_ref, kn_ref, vn_ref, ck_ref, cv_ref, bias_ref, dl_ref, g_ref, o_ref,
                          *, heads, past, scale, lam_init):
    lam = _diff_lambda(dl_ref, lam_init)
    for h in range(heads):
        hs = slice(h * HEAD_LANES, (h + 1) * HEAD_LANES)
        q1, q2 = _split_q(q_ref[:, hs], HEAD_LANES // 2)
        kp = ck_ref[0, :, hs].astype(BF16)
        kn = kn_ref[:, hs].astype(BF16)
        bp = bias_ref[h, :, 0:past]
        bn = bias_ref[h, :, past:]
        p1 = _softmax_parts([_dot_nt(q1, kp) * scale + bp, _dot_nt(q1, kn) * scale + bn])
        p2 = _softmax_parts([_dot_nt(q2, kp) * scale + bp, _dot_nt(q2, kn) * scale + bn])
        o = (_dot((p1[0] - lam * p2[0]).astype(BF16), cv_ref[0, :, hs].astype(BF16))
             + _dot((p1[1] - lam * p2[1]).astype(BF16), vn_ref[:, hs].astype(BF16)))
        o_ref[:, hs] = (_rms(o, g_ref[...]) * (1.0 - lam_init)).astype(o_ref.dtype)


def attn_a_sample(cfg, z, cache_k, cache_v, layer, bias, dl, g, lam_init):
    Bd, T, H, P = cfg.dec_batch, cfg.dec_seq, cfg.a_heads, cfg.past_len
    W = cfg.a_width
    r0 = cfg.rows_p // T
    new = lambda c: pl.BlockSpec((T, W), lambda b: (r0 + b, c))
    cache = pl.BlockSpec((None, 1, P, W), lambda b: (layer, b, 0, 0))
    return pl.pallas_call(
        functools.partial(_attn_a_sample_kernel, heads=H, past=P, scale=cfg.a_qk ** -0.5, lam_init=lam_init),
        out_shape=jax.ShapeDtypeStruct((Bd * T, W), BF16),
        grid=(Bd,),
        in_specs=[new(0), new(1), new(2), cache, cache,
                  pl.BlockSpec(bias.shape, lambda b: (0, 0, 0)),
                  pl.BlockSpec(dl.shape, lambda b: (0, 0)),
                  pl.BlockSpec((1, HEAD_LANES), lambda b: (0, 0))],
        out_specs=pl.BlockSpec((T, W), lambda b: (b, 0)),
        compiler_params=_cparams(1, VMEM_LIMIT_V7X), name="attn_a_sample",
    )(z, z, z, cache_k, cache_v, bias, dl, g.reshape(1, HEAD_LANES))


def _attn_b_prompt_kernel(q_ref, k0, k1, k2, v0, v1, v2, bias_ref, o_ref, *, bq, scale):
    i = pl.program_id(2)
    q = q_ref[...].astype(BF16)
    parts = []
    for j, kr in enumerate((k0, k1, k2)):
        pen = jnp.where(i - 2 + j >= 0, 0.0, NEG)
        parts.append(_dot_nt(q, kr[...].astype(BF16)) * scale + bias_ref[0, :, j * bq:(j + 1) * bq] + pen)
    ps = _softmax_parts(parts)
    o = functools.reduce(jnp.add, [_dot(p.astype(BF16), vr[...].astype(BF16)) for p, vr in zip(ps, (v0, v1, v2))])
    o_ref[...] = o.astype(o_ref.dtype)


def attn_b_prompt(cfg, z, bias):
    B, S, H, bq = cfg.batch, cfg.seq, cfg.b_heads, cfg.bq_b
    nq = S // bq
    c0 = 3 * cfg.a_width // HEAD_LANES

    def kv(col, j):
        return pl.BlockSpec((bq, HEAD_LANES),
                            lambda b, h, i: (b * nq + jnp.maximum(i - 2 + j, 0), c0 + col * H + h))

    return pl.pallas_call(
        functools.partial(_attn_b_prompt_kernel, bq=bq, scale=HEAD_LANES ** -0.5),
        out_shape=jax.ShapeDtypeStruct((B * S, cfg.b_width), BF16),
        grid=(B, H, nq),
        in_specs=[pl.BlockSpec((bq, HEAD_LANES), lambda b, h, i: (b * nq + i, c0 + h)),
                  kv(1, 0), kv(1, 1), kv(1, 2), kv(2, 0), kv(2, 1), kv(2, 2),
                  pl.BlockSpec((1, bq, 3 * bq), lambda b, h, i: (h, 0, 0))],
        out_specs=pl.BlockSpec((bq, HEAD_LANES), lambda b, h, i: (b * nq + i, h)),
        compiler_params=_cparams(3, VMEM_LIMIT_V7X), name="attn_b_prompt",
    )(z, z, z, z, z, z, z, bias)


def _attn_b_sample_kernel(q_ref, kn_ref, vn_ref, ck_ref, cv_ref, bias_ref, o_ref, *, heads, past, scale):
    for h in range(heads):
        hs = slice(h * HEAD_LANES, (h + 1) * HEAD_LANES)
        q = q_ref[:, hs].astype(BF16)
        ps = _softmax_parts([_dot_nt(q, ck_ref[0, :, hs].astype(BF16)) * scale + bias_ref[h, :, 0:past],
                             _dot_nt(q, kn_ref[:, hs].astype(BF16)) * scale + bias_ref[h, :, past:]])
        o = (_dot(ps[0].astype(BF16), cv_ref[0, :, hs].astype(BF16))
             + _dot(ps[1].astype(BF16), vn_ref[:, hs].astype(BF16)))
        o_ref[:, hs] = o.astype(o_ref.dtype)


def attn_b_sample(cfg, z, cache_k, cache_v, layer, bias):
    Bd, T, H = cfg.dec_batch, cfg.dec_seq, cfg.b_heads
    Pc = cache_k.shape[2]
    W = cfg.b_width
    r0 = cfg.rows_p // T
    c0 = 3 * cfg.a_width // W
    new = lambda c: pl.BlockSpec((T, W), lambda b: (r0 + b, c0 + c))
    cache = pl.BlockSpec((None, 1, Pc, W), lambda b: (layer, b, 0, 0))
    return pl.pallas_call(
        functools.partial(_attn_b_sample_kernel, heads=H, past=Pc, scale=HEAD_LANES ** -0.5),
        out_shape=jax.ShapeDtypeStruct((Bd * T, W), BF16),
        grid=(Bd,),
        in_specs=[new(0), new(1), new(2), cache, cache, pl.BlockSpec(bias.shape, lambda b: (0, 0, 0))],
        out_specs=pl.BlockSpec((T, W), lambda b: (b, 0)),
        compiler_params=_cparams(1, VMEM_LIMIT_V7X), name="attn_b_sample",
    )(z, z, z, cache_k, cache_v, bias)


def _split3(x):
    hi = x.astype(BF16)
    r = x - hi.astype(F32)
    mid = r.astype(BF16)
    lo = (r - mid.astype(F32)).astype(BF16)
    return hi, mid, lo


def _hgrn_head(qz, fz, vv, lb, st, L):
    sig = jax.nn.sigmoid(fz)
    logf = jnp.log(lb + (1.0 - lb) * sig)
    kk = (1.0 - lb) * (1.0 - sig)
    qq = _silu(qz)
    row = lax.broadcasted_iota(jnp.int32, (L, 1), 0)
    col = lax.broadcasted_iota(jnp.int32, (1, L), 1)
    tril = (col <= row).astype(BF16)
    b = functools.reduce(jnp.add, [_dot(tril, part) for part in _split3(logf)])
    vb = vv.astype(BF16)

    o = _dot_nt((qq * jnp.exp(b)).astype(BF16), st.astype(BF16))
    b_last = b[L - 1:L, :]
    kd = kk * jnp.exp(b_last - b)
    st_new = jnp.exp(b_last) * st + _dot_tn(vb, kd.astype(BF16))

    if L > SUB:
        p_off = jnp.zeros((L, L), F32)
        m = SUB
        while m < L:
            is_q = (row // m) % 2 == 1
            ref = b[m - 1:m, :]
            for g in range(1, L // (2 * m)):
                ref = jnp.where(row >= g * 2 * m, b[g * 2 * m + m - 1:g * 2 * m + m, :], ref)
            d = b - ref
            x = (jnp.where(is_q, qq, kk) * jnp.exp(jnp.where(is_q, d, -d))).astype(BF16)
            pair = jnp.logical_and(is_q, col // m == row // m - 1)
            p_off = p_off + jnp.where(pair, _dot_nt(x, x), 0.0)
            m *= 2
        o = o + _dot(p_off.astype(BF16), vb)

    ones = jnp.ones((HEAD_LANES, HEAD_LANES), BF16)
    t_idx = lax.broadcasted_iota(jnp.int32, (SUB, 1), 0)
    diag = []
    for r0 in range(0, L, SUB):
        bj, qj, kj, vj = (a[r0:r0 + SUB, :] for a in (b, qq, kk, vv))
        w = [jnp.exp(jnp.where(t_idx >= s, bj - bj[s:s + 1, :], NEG)) * qj * kj[s:s + 1, :] for s in range(SUB)]
        r = _dot(jnp.concatenate(w, axis=0).astype(BF16), ones)
        diag.append(functools.reduce(jnp.add, [r[s * SUB:(s + 1) * SUB, :] * vj[s:s + 1, :] for s in range(SUB)]))
    o = o + (jnp.concatenate(diag, axis=0) if len(diag) > 1 else diag[0])
    return o, st_new


def _hgrn_kernel(q_ref, f_ref, i_ref, g_ref, lb_ref, ng_ref, s0_ref, o_ref, so_ref, st_ref, *, heads, L):
    c = pl.program_id(1)

    @pl.when(c == 0)
    def _():
        st_ref[...] = s0_ref[0]

    for h in range(heads):
        hs = slice(h * HEAD_LANES, (h + 1) * HEAD_LANES)
        o, st_new = _hgrn_head(q_ref[:, hs], f_ref[:, hs], i_ref[:, hs], lb_ref[:, hs], st_ref[h], L)
        st_ref[h] = st_new
        o_ref[:, hs] = (_rms(o, ng_ref[...]) * _silu(g_ref[:, hs])).astype(o_ref.dtype)

    @pl.when(c == pl.num_programs(1) - 1)
    def _():
        so_ref[0] = st_ref[...]


def hgrn2(cfg, z, lb, norm_g, s0t, *, nb, T, row0):
    H, W = cfg.c_heads, cfg.c_width
    L = min(cfg.chunk, T)
    nc = T // L
    c0 = (3 * cfg.a_width + 3 * cfg.b_width) // W
    r0 = row0 // L
    col = lambda c: pl.BlockSpec((L, W), lambda b, i: (r0 + b * nc + i, c0 + c))
    state = pl.BlockSpec((1, H, HEAD_LANES, HEAD_LANES), lambda b, i: (b, 0, 0, 0))
    return pl.pallas_call(
        functools.partial(_hgrn_kernel, heads=H, L=L),
        out_shape=[jax.ShapeDtypeStruct((nb * T, W), BF16),
                   jax.ShapeDtypeStruct((nb, H, HEAD_LANES, HEAD_LANES), F32)],
        grid=(nb, nc),
        in_specs=[col(0), col(1), col(2), col(3),
                  pl.BlockSpec((1, W), lambda b, i: (0, 0)),
                  pl.BlockSpec((1, HEAD_LANES), lambda b, i: (0, 0)),
                  state],
        out_specs=[pl.BlockSpec((L, W), lambda b, i: (b * nc + i, 0)), state],
        scratch_shapes=[pltpu.VMEM((H, HEAD_LANES, HEAD_LANES), F32)],
        compiler_params=_cparams(2, VMEM_LIMIT_V7X), name="hgrn2",
    )(z, z, z, z, lb.reshape(1, W), norm_g.reshape(1, HEAD_LANES), s0t)


def _t5_bucket(cfg, rel):
    half = cfg.t5_buckets // 2
    n = -rel
    ret = jnp.where(n < 0, half, 0)
    n = jnp.abs(n)
    max_exact = half // 2
    nf = jnp.maximum(n, 1).astype(F32)
    large = max_exact + (jnp.log(nf / max_exact) / math.log(cfg.t5_max_dist / max_exact)
                         * (half - max_exact)).astype(jnp.int32)
    large = jnp.minimum(large, half - 1)
    return ret + jnp.where(n < max_exact, n, large)


def _toeplitz(v, nq, nk):
    lv = v.shape[-1]
    assert lv >= nq + nk - 1
    lead = v.shape[:-1]
    t = jnp.tile(v, (1,) * len(lead) + (nq + 1,))[..., :nq * (lv + 1)].reshape(lead + (nq, lv + 1))
    return t[..., :nk][..., ::-1]


def _t5_tables(cfg, t5_table):
    bq = cfg.bq_a
    rel = bq - 1 - jnp.arange(3 * bq - 1)
    near = _toeplitz(t5_table[_t5_bucket(cfg, rel)].T, bq, 2 * bq)
    far = t5_table[_t5_bucket(cfg, jnp.array([-2 * bq]))].T[:, :, None]
    qpos = np.arange(bq)[:, None]
    kpos = np.arange(-bq, bq)[None, :]
    visible = (kpos // cfg.chunk) <= (qpos // cfg.chunk)
    near = jnp.where(visible[None], (near - far) * LOG2E, NEG)
    sq = cfg.past_len + jnp.arange(cfg.dec_seq)
    sk = jnp.arange(cfg.past_len + cfg.dec_seq)
    sample = jnp.moveaxis(t5_table[_t5_bucket(cfg, sk[None, :] - sq[:, None])], -1, 0)
    return near.astype(F32), sample.astype(F32)


def _band_tables(cfg, rel_table, n_cache):
    bq = cfg.bq_b
    clipped = lambda dist: jnp.clip(dist, -cfg.b_max_dist, cfg.b_max_dist) + cfg.b_max_dist
    dist = jnp.arange(4 * bq - 1) - bq + 1
    prompt = _toeplitz(rel_table[..., clipped(dist)], bq, 3 * bq)
    qc = np.arange(bq)[:, None] // cfg.chunk
    kc = np.arange(-2 * bq, bq)[None, :] // cfg.chunk
    band = (kc <= qc) & (kc >= qc - cfg.b_left_chunks)
    prompt = jnp.where(band[None, None], prompt, NEG)
    sq = cfg.past_len + jnp.arange(cfg.dec_seq)
    kbpos = jnp.concatenate([cfg.past_len - n_cache + jnp.arange(n_cache), sq])
    sample = rel_table[..., clipped(sq[:, None] - kbpos[None, :])]
    return prompt.astype(F32), sample.astype(F32)


def _check(cfg):
    assert cfg.rows_s % 16 == 0 and cfg.seq % cfg.rows_s == 0
    assert cfg.rows % cfg.bm == 0 and cfg.bm % 16 == 0
    assert cfg.d_model % cfg.bn == 0 and (cfg.n_in - 3 * cfg.d_model) % cfg.bn == 0 and cfg.n_in % cfg.bn == 0
    assert cfg.n_in % cfg.bn_wide == 0 and cfg.d_model % cfg.bn_wide == 0
    assert (3 * cfg.a_width + 3 * cfg.b_width) % cfg.c_width == 0 and (3 * cfg.a_width) % cfg.b_width == 0
    assert cfg.seq % cfg.bq_a == 0 and cfg.bq_a % cfg.chunk == 0 and cfg.bq_a >= cfg.t5_max_dist
    assert cfg.seq % cfg.bq_b == 0 and cfg.bq_b % cfg.chunk == 0 and 2 * cfg.bq_b >= cfg.b_win
    assert cfg.seq % cfg.chunk == 0 and cfg.chunk % SUB == 0 and cfg.dec_seq % SUB == 0
    assert cfg.past_len % cfg.chunk == 0 and cfg.dec_seq <= cfg.chunk
    assert min(cfg.b_win, cfg.past_len) <= cfg.b_win and cfg.rows_p % cfg.dec_seq == 0


def step(cfg, x_prompt, x_sample, cache_a_k, cache_a_v, cache_b_k, cache_b_v, state_c, c_prompt, c_sample,
         w_ada, b_ada, w_in, diff_lambda, a_subln_g, t5_bias, b_rel_bias, c_lb_param, c_norm_g,
         w_branch_a, w_branch_b, w_branch_c, w_o, ln1_g, ln1_b, ln2_g, ln2_b, w_up, w_down):
    _check(cfg)
    D, B, S, Bd, T = cfg.d_model, cfg.batch, cfg.seq, cfg.dec_batch, cfg.dec_seq
    RP = cfg.rows_p
    n_ctx = B + Bd
    pad = (-n_ctx) % 8
    c_all = jnp.concatenate([c_prompt, c_sample, jnp.zeros((pad, D), F32)], axis=0)
    mod = adaln_all(cfg, c_all, w_ada, b_ada)

    lb_soft = jax.nn.softmax(c_lb_param.astype(F32), axis=0)
    lb_all = jnp.cumsum(lb_soft, axis=0) - lb_soft[0]
    a_near, a_sample = _t5_tables(cfg, t5_bias)
    n_cache = cache_b_k.shape[2]
    b_prompt, b_sample = _band_tables(cfg, b_rel_bias, n_cache)
    zero_state = jnp.zeros((B, cfg.c_heads, HEAD_LANES, HEAD_LANES), F32)
    A, Bw = cfg.a_width, cfg.b_width
    cak, cav = (c.reshape(cfg.depth, Bd, cfg.past_len, A) for c in (cache_a_k, cache_a_v))
    cbk, cbv = (c.reshape(cfg.depth, Bd, n_cache, Bw) for c in (cache_b_k, cache_b_v))
    n_b_rows = min(cfg.b_win, S)

    def heads(a, lead):
        return a.reshape(lead + (a.shape[-1] // HEAD_LANES, HEAD_LANES))

    x, h = rowwise(cfg, (x_prompt.reshape(RP, D), x_sample.reshape(Bd * T, D)), mod_next=mod[0], sc_col=1, sh_col=0)
    outs = [[] for _ in range(10)]
    for l in range(cfg.depth):
        lam_init = 0.8 - 0.6 * math.exp(-0.3 * l)
        z = matmul(cfg, h, w_in, l, bn=cfg.bn_wide, name="w_in")

        oa = jnp.concatenate([attn_a_prompt(cfg, z, a_near, diff_lambda[l], a_subln_g[l], lam_init),
                              attn_a_sample(cfg, z, cak, cav, l, a_sample, diff_lambda[l], a_subln_g[l], lam_init)],
                             axis=0)
        ob = jnp.concatenate([attn_b_prompt(cfg, z, b_prompt[l]), attn_b_sample(cfg, z, cbk, cbv, l, b_sample[l])],
                             axis=0)
        oc_p, st_p = hgrn2(cfg, z, lb_all[l], c_norm_g[l], zero_state, nb=B, T=S, row0=0)
        oc_s, st_s = hgrn2(cfg, z, lb_all[l], c_norm_g[l], jnp.swapaxes(state_c[l], -1, -2), nb=Bd, T=T, row0=RP)
        oc = jnp.concatenate([oc_p, oc_s], axis=0)

        mixed = branch_mix(cfg, oa, ob, oc, z, w_branch_a, w_branch_b, w_branch_c, l)
        y = matmul(cfg, mixed, w_o, l, bn=cfg.bn_wide, name="w_o")
        x, h2 = rowwise(cfg, x, y, mod[l], ln1_g[l], ln1_b[l], mod[l], g_col=2, sc_col=4, sh_col=3)
        u = matmul(cfg, h2, w_up, l, bn=cfg.bn_wide, out_dtype=BF16, relu2=True, name="w_up")
        m = matmul(cfg, u, w_down, l, bn=cfg.bn, bk=D, name="w_down")
        if l + 1 < cfg.depth:
            x, h = rowwise(cfg, x, m, mod[l], ln2_g[l], ln2_b[l], mod[l + 1], g_col=5, sc_col=1, sh_col=0)
        else:
            x_p, x_s = rowwise(cfg, x, m, mod[l], ln2_g[l], ln2_b[l], split_out=True, g_col=5)

        kb0, vb0 = 3 * A + Bw, 3 * A + 2 * Bw
        tail = lambda c0: jnp.stack([z[(b + 1) * S - n_b_rows:(b + 1) * S, c0:c0 + Bw] for b in range(B)])
        outs[0].append(heads(z[:RP, A:2 * A], (B, S)))
        outs[1].append(heads(z[:RP, 2 * A:3 * A], (B, S)))
        outs[2].append(heads(tail(kb0), (B, n_b_rows)))
        outs[3].append(heads(tail(vb0), (B, n_b_rows)))
        outs[4].append(jnp.swapaxes(st_p, -1, -2))
        outs[5].append(heads(z[RP:, A:2 * A], (Bd, T)))
        outs[6].append(heads(z[RP:, 2 * A:3 * A], (Bd, T)))
        outs[7].append(heads(z[RP:, kb0:kb0 + Bw], (Bd, T)))
        outs[8].append(heads(z[RP:, vb0:vb0 + Bw], (Bd, T)))
        outs[9].append(jnp.swapaxes(st_s, -1, -2))

    return (x_p.reshape(B, S, D), x_s.reshape(Bd, T, D)) + tuple(jnp.stack(o) for o in outs)


def kernel(x_prompt, x_sample, cache_a_k, cache_a_v, cache_b_k, cache_b_v, state_c, c_prompt, c_sample, w_ada, b_ada, w_in, diff_lambda, a_subln_g, t5_bias, b_rel_bias, c_lb_param, c_norm_g, w_branch_a, w_branch_b, w_branch_c, w_o, ln1_g, ln1_b, ln2_g, ln2_b, w_up, w_down):
    return step(Cfg(), x_prompt, x_sample, cache_a_k, cache_a_v, cache_b_k, cache_b_v, state_c, c_prompt, c_sample,
                w_ada, b_ada, w_in, diff_lambda, a_subln_g, t5_bias, b_rel_bias, c_lb_param, c_norm_g,
                w_branch_a, w_branch_b, w_branch_c, w_o, ln1_g, ln1_b, ln2_g, ln2_b, w_up, w_down)
```

```python
import functools
import math
from typing import NamedTuple

import jax
import jax.numpy as jnp
import numpy as np
from jax import lax
from jax.experimental import pallas as pl
from jax.experimental.pallas import tpu as pltpu

F32 = jnp.float32
BF16 = jnp.bfloat16
NEG = -1e30
LOG2E = math.log2(math.e)
LN_EPS = 1e-5
RMS_EPS = 1e-6
HEAD_LANES = 128
SUB = 16
VMEM_LIMIT_V7X = 56 * 1024 * 1024


class Cfg(NamedTuple):
    d_model: int = 4096
    batch: int = 2
    seq: int = 4096
    depth: int = 4
    dec_batch: int = 8
    dec_seq: int = 16
    past_len: int = 1024
    chunk: int = 64
    a_heads: int = 12
    a_qk: int = 64
    b_heads: int = 12
    b_left_chunks: int = 8
    b_max_dist: int = 128
    c_heads: int = 8
    t5_buckets: int = 32
    t5_max_dist: int = 128
    bm: int = 640
    bn: int = 512
    bn_wide: int = 1024
    bq_a: int = 512
    bq_b: int = 256
    b_group: int = 4

    @property
    def a_width(self): return self.a_heads * HEAD_LANES
    @property
    def b_width(self): return self.b_heads * HEAD_LANES
    @property
    def c_width(self): return self.c_heads * HEAD_LANES
    @property
    def d_ff(self): return 4 * self.d_model
    @property
    def b_win(self): return self.b_left_chunks * self.chunk
    @property
    def n_in(self): return 3 * self.a_width + 3 * self.b_width + 4 * self.c_width + 3 * self.d_model
    @property
    def rows_p(self): return self.batch * self.seq
    @property
    def rows_s(self): return self.dec_batch * self.dec_seq
    @property
    def rows(self): return self.rows_p + self.rows_s
    @property
    def alpha(self): return (2 * self.depth) ** 0.25


def _cparams(n_axes, vmem=None):
    return pltpu.CompilerParams(dimension_semantics=("arbitrary",) * n_axes, vmem_limit_bytes=vmem)


def _dot(a, b):
    return jnp.dot(a, b, preferred_element_type=F32)


def _dot_nt(a, b):
    return lax.dot_general(a, b, (((1,), (1,)), ((), ())), preferred_element_type=F32)


def _dot_tn(a, b):
    return lax.dot_general(a, b, (((0,), (0,)), ((), ())), preferred_element_type=F32)


def _silu(x):
    return x * jax.nn.sigmoid(x)


def _cast_rows(dst_ref, src_ref, rows_per_step=512):
    n = src_ref.shape[0]
    step = min(rows_per_step, n)
    for r in range(0, n, step):
        dst_ref[r:r + step, :] = src_ref[r:r + step, :].astype(BF16)


def _adaln_kernel(c_ref, w_ref, b_ref, o_ref):
    s = _silu(c_ref[...]).astype(BF16)
    o_ref[0] = _dot(s, w_ref[0].astype(BF16)) + b_ref[0]


def adaln_all(cfg, c_all, w_ada, b_ada):
    R, D = c_all.shape
    L, _, N = w_ada.shape
    bn = cfg.bn
    return pl.pallas_call(
        _adaln_kernel,
        out_shape=jax.ShapeDtypeStruct((L, R, N), F32),
        grid=(L, N // bn),
        in_specs=[pl.BlockSpec((R, D), lambda l, j: (0, 0)),
                  pl.BlockSpec((1, D, bn), lambda l, j: (l, 0, j)),
                  pl.BlockSpec((1, 1, bn), lambda l, j: (l, 0, j))],
        out_specs=pl.BlockSpec((1, R, bn), lambda l, j: (l, 0, j)),
        compiler_params=_cparams(2, VMEM_LIMIT_V7X),
        name="adaln",
    )(c_all, w_ada, b_ada.reshape(L, 1, N))


def _rowwise_kernel(*refs, cfg, n_x, n_xo, do_ln, do_mod, g_col, sc_col, sh_col):
    refs = list(refs)
    x_refs = [refs.pop(0) for _ in range(n_x)]
    y_ref = refs.pop(0) if do_ln else None
    modg_ref = refs.pop(0) if do_ln else None
    lng_ref = refs.pop(0) if do_ln else None
    lnb_ref = refs.pop(0) if do_ln else None
    modn_ref = refs.pop(0) if do_mod else None
    xo_refs = [refs.pop(0) for _ in range(n_xo)]
    ho_ref = refs.pop(0) if do_mod else None
    D = cfg.d_model
    bm = cfg.rows_s
    n_prompt_blocks = cfg.rows_p // bm
    blocks_per_batch = cfg.seq // bm
    i = pl.program_id(0)

    def slab(x_ref, xo_ref, r0, nrows, brow):
        x = x_ref[r0:r0 + nrows, :]
        if do_ln:
            g = modg_ref[pl.ds(brow, 1), g_col * D:(g_col + 1) * D]
            u = cfg.alpha * x + (1.0 + g) * y_ref[r0:r0 + nrows, :]
            mu = jnp.mean(u, axis=-1, keepdims=True)
            var = jnp.mean(jnp.square(u - mu), axis=-1, keepdims=True)
            x = (u - mu) * lax.rsqrt(var + LN_EPS) * lng_ref[...] + lnb_ref[...]
        if xo_ref is not None:
            xo_ref[r0:r0 + nrows, :] = x
        if do_mod:
            sc = modn_ref[pl.ds(brow, 1), sc_col * D:(sc_col + 1) * D]
            sh = modn_ref[pl.ds(brow, 1), sh_col * D:(sh_col + 1) * D]
            ho_ref[r0:r0 + nrows, :] = (x * (1.0 + sc) + sh).astype(BF16)

    @pl.when(i < n_prompt_blocks)
    def _():
        slab(x_refs[0], xo_refs[0] if xo_refs else None, 0, bm, i // blocks_per_batch)

    @pl.when(i >= n_prompt_blocks)
    def _():
        for s in range(cfg.dec_batch):
            slab(x_refs[-1], xo_refs[-1] if xo_refs else None, s * cfg.dec_seq, cfg.dec_seq, cfg.batch + s)


def rowwise(cfg, x, y=None, mod_gate=None, ln_g=None, ln_b=None, mod_next=None, *, split_out=False,
            g_col=0, sc_col=0, sh_col=0):
    do_ln = y is not None
    do_mod = mod_next is not None
    xs = list(x) if isinstance(x, (tuple, list)) else [x]
    D = cfg.d_model
    M, bm = cfg.rows, cfg.rows_s
    n_prompt_blocks = cfg.rows_p // bm
    row = pl.BlockSpec((bm, D), lambda i: (i, 0))
    row_p = pl.BlockSpec((bm, D), lambda i: (jnp.minimum(i, n_prompt_blocks - 1), 0))
    row_s = pl.BlockSpec((bm, D), lambda i: (0, 0))
    full = lambda a: pl.BlockSpec(a.shape, lambda i: (0,) * a.ndim)
    args = list(xs)
    specs = [row] if len(xs) == 1 else [row_p, row_s]
    outs, out_specs = [], []
    if do_ln:
        ln_g = ln_g.reshape(1, D)
        ln_b = ln_b.reshape(1, D)
        args += [y, mod_gate, ln_g, ln_b]
        specs += [row, full(mod_gate), full(ln_g), full(ln_b)]
    if do_mod:
        args.append(mod_next)
        specs.append(full(mod_next))
    n_xo = 0
    if do_ln or len(xs) == 2:
        if split_out:
            outs += [jax.ShapeDtypeStruct((cfg.rows_p, D), F32), jax.ShapeDtypeStruct((cfg.rows_s, D), F32)]
            out_specs += [row_p, row_s]
        else:
            outs.append(jax.ShapeDtypeStruct((M, D), F32))
            out_specs.append(row)
        n_xo = len(outs)
    if do_mod:
        outs.append(jax.ShapeDtypeStruct((M, D), BF16))
        out_specs.append(row)
    return pl.pallas_call(
        functools.partial(_rowwise_kernel, cfg=cfg, n_x=len(xs), n_xo=n_xo, do_ln=do_ln, do_mod=do_mod,
                          g_col=g_col, sc_col=sc_col, sh_col=sh_col),
        out_shape=outs, grid=(M // bm,), in_specs=specs, out_specs=out_specs,
        compiler_params=_cparams(1, VMEM_LIMIT_V7X), name="rowwise",
    )(*args)


def _mm_kernel(x_ref, w_ref, o_ref, wbf_ref, *acc, nk, nch, ck, n_wsteps, relu2):
    t = pl.program_id(0)
    i = pl.program_id(1)

    @pl.when(jnp.logical_and(t < n_wsteps, i < nch))
    def _():
        wbf_ref[t % 2, pl.ds(pl.multiple_of(i * ck, ck), ck), :] = w_ref[...].astype(BF16)

    @pl.when(t >= 1)
    def _():
        s = t - 1
        d = _dot(x_ref[...], wbf_ref[s % 2])

        def finish(r):
            if relu2:
                r = jnp.square(jnp.maximum(r, 0.0))
            o_ref[...] = r.astype(o_ref.dtype)

        if nk == 1:
            finish(d)
        else:
            acc_ref = acc[0]
            k = s % nk

            @pl.when(k == 0)
            def _():
                acc_ref[i] = d

            @pl.when(jnp.logical_and(k > 0, k < nk - 1))
            def _():
                acc_ref[i] += d

            @pl.when(k == nk - 1)
            def _():
                finish(acc_ref[i] + d)


def matmul(cfg, x, w, layer, *, bn, bm=None, row_passes=1, out_dtype=F32, relu2=False, bk=None, name="matmul"):
    M, K = x.shape
    N = w.shape[-1]
    bm = cfg.bm if bm is None else bm
    bk = K if bk is None else bk
    assert N % bn == 0 and M % (bm * row_passes) == 0 and K % bk == 0
    nk, nm = K // bk, M // bm // row_passes
    nch = max(c for c in (1, 2, 4, 8) if c <= nm and bk % (16 * c) == 0)
    ck = bk // nch
    S = (N // bn) * nk
    S_all = S * row_passes

    def x_map(t, i):
        s = jnp.maximum(t - 1, 0)
        return (jnp.where(t == 0, 0, (s // S) * nm + i), s % nk)

    def w_map(t, i):
        wq = jnp.minimum(t, S_all - 1) % S
        c = jnp.where(t < S_all, jnp.minimum(i, nch - 1), nch - 1)
        return (layer, (wq % nk) * nch + c, wq // nk)

    def o_map(t, i):
        s = jnp.maximum(t - 1, 0)
        writes = jnp.logical_and(t >= 1, s % nk == nk - 1)
        return ((s // S) * nm + jnp.where(writes, i, 0), (s % S) // nk)

    scratch = [pltpu.VMEM((2, bk, bn), BF16)]
    if nk > 1:
        scratch.append(pltpu.VMEM((nm, bm, bn), F32))
    return pl.pallas_call(
        functools.partial(_mm_kernel, nk=nk, nch=nch, ck=ck, n_wsteps=S_all, relu2=relu2),
        out_shape=jax.ShapeDtypeStruct((M, N), out_dtype),
        grid=(S_all + 1, nm),
        in_specs=[pl.BlockSpec((bm, bk), x_map), pl.BlockSpec((None, ck, bn), w_map)],
        out_specs=pl.BlockSpec((bm, bn), o_map),
        scratch_shapes=scratch,
        compiler_params=_cparams(2, VMEM_LIMIT_V7X), name=name,
    )(x, w)


def _mix_kernel(oa_ref, ob_ref, oc_ref, ga_ref, gb_ref, gc_ref, wa_ref, wb_ref, wc_ref, o_ref, wa_s, wb_s, wc_s):
    @pl.when(pl.program_id(1) == 0)
    def _():
        _cast_rows(wa_s, wa_ref)
        _cast_rows(wb_s, wb_ref)
        _cast_rows(wc_s, wc_ref)

    mixed = (jax.nn.sigmoid(ga_ref[...]) * _dot(oa_ref[...], wa_s[...])
             + jax.nn.sigmoid(gb_ref[...]) * _dot(ob_ref[...], wb_s[...])
             + jax.nn.sigmoid(gc_ref[...]) * _dot(oc_ref[...], wc_s[...]))
    o_ref[...] = mixed.astype(o_ref.dtype)


def branch_mix(cfg, oa, ob, oc, z, wa, wb, wc, layer):
    M = oa.shape[0]
    D = cfg.d_model
    bm, bn = cfg.bm, cfg.bn
    g0 = (3 * cfg.a_width + 3 * cfg.b_width + 4 * cfg.c_width) // bn
    gstep = D // bn
    act = lambda w: pl.BlockSpec((bm, w), lambda j, i: (i, 0))
    gate = lambda t: pl.BlockSpec((bm, bn), lambda j, i: (i, g0 + t * gstep + j))
    wgt = lambda w: pl.BlockSpec((None, w, bn), lambda j, i: (layer, 0, j))
    return pl.pallas_call(
        _mix_kernel,
        out_shape=jax.ShapeDtypeStruct((M, D), BF16),
        grid=(D // bn, M // bm),
        in_specs=[act(cfg.a_width), act(cfg.b_width), act(cfg.c_width), gate(0), gate(1), gate(2),
                  wgt(cfg.a_width), wgt(cfg.b_width), wgt(cfg.c_width)],
        out_specs=pl.BlockSpec((bm, bn), lambda j, i: (i, j)),
        scratch_shapes=[pltpu.VMEM((cfg.a_width, bn), BF16), pltpu.VMEM((cfg.b_width, bn), BF16),
                        pltpu.VMEM((cfg.c_width, bn), BF16)],
        compiler_params=_cparams(2, VMEM_LIMIT_V7X), name="branch_mix",
    )(oa, ob, oc, z, z, z, wa, wb, wc)


def _diff_lambda(dl_ref, lam_init):
    dl = dl_ref[...]
    s1 = jnp.sum(dl[0:1] * dl[1:2], axis=-1, keepdims=True)
    s2 = jnp.sum(dl[2:3] * dl[3:4], axis=-1, keepdims=True)
    return jnp.exp(s1) - jnp.exp(s2) + lam_init


def _split_q(q, half):
    lane = lax.broadcasted_iota(jnp.int32, q.shape, 1)
    return (jnp.where(lane < half, q, 0.0).astype(BF16), jnp.where(lane >= half, q, 0.0).astype(BF16))


def _rms(o, g):
    return o * lax.rsqrt(jnp.mean(jnp.square(o), axis=-1, keepdims=True) + RMS_EPS) * g


def _flash_step(s, v, st):
    m, l, acc = st
    m_new = jnp.maximum(m, jnp.max(s, axis=-1, keepdims=True))
    a = jnp.exp2(m - m_new)
    p = jnp.exp2(s - m_new)
    return (m_new, a * l + jnp.sum(p, axis=-1, keepdims=True), a * acc + _dot(p.astype(BF16), v))


def _chunk_of(pos, chunk):
    return lax.shift_right_arithmetic(pos, int(math.log2(chunk)))


def _attn_a_prompt_kernel(q_ref, k_ref, v_ref, wt5_ref, dl_ref, g_ref, o_ref, kbf, vbf, bias_s,
                          *, bq, chunk, scale, lam_init):
    qb = pl.program_id(2)

    @pl.when(qb == 0)
    def _():
        _cast_rows(kbf, k_ref)
        _cast_rows(vbf, v_ref)
        r = lax.broadcasted_iota(jnp.int32, (bq, 2 * bq), 0)
        c = lax.broadcasted_iota(jnp.int32, (bq, 2 * bq), 1)
        visible = _chunk_of(c - bq, chunk) <= _chunk_of(r, chunk)
        bias_s[...] = jnp.where(visible, _toeplitz_tile(wt5_ref[0], bq, 2 * bq), NEG)

    q1, q2 = _split_q(q_ref[...] * (scale * LOG2E), HEAD_LANES // 2)

    def block(koff, bias, st):
        k = kbf[pl.ds(koff, bq), :]
        v = vbf[pl.ds(koff, bq), :]
        s1, s2 = _dot_nt(q1, k), _dot_nt(q2, k)
        if bias is not None:
            s1, s2 = s1 + bias, s2 + bias
        return (_flash_step(s1, v, st[0]), _flash_step(s2, v, st[1]))

    init = (jnp.full((bq, 1), NEG, F32), jnp.zeros((bq, 1), F32), jnp.zeros((bq, HEAD_LANES), F32))
    st = lax.fori_loop(0, jnp.maximum(qb - 1, 0),
                       lambda kb, st: block(pl.multiple_of(kb * bq, bq), None, st), (init, init))
    prev_pen = jnp.where(qb > 0, 0.0, NEG)
    st = block(pl.multiple_of(jnp.maximum(qb - 1, 0) * bq, bq), bias_s[:, 0:bq] + prev_pen, st)
    st = block(pl.multiple_of(qb * bq, bq), bias_s[:, bq:2 * bq], st)
    (_, l1, a1), (_, l2, a2) = st
    lam = _diff_lambda(dl_ref, lam_init)
    o = a1 / l1 - lam * (a2 / l2)
    o_ref[...] = (_rms(o, g_ref[...]) * (1.0 - lam_init)).astype(o_ref.dtype)


def _toeplitz_tile(w_row, nq, nk):
    x = jnp.broadcast_to(w_row, (nq, w_row.shape[-1]))
    return pltpu.roll(x, 0, 1, stride=1, stride_axis=0)[:, :nk]


def _toeplitz_offsets(nq, nk):
    W = -(-(nq + nk - 1) // HEAD_LANES) * HEAD_LANES
    j = np.arange(W)
    return W, np.where(j < nk, j, j - W)


def attn_a_prompt(cfg, z, wt5, dl, g, lam_init):
    B, S, H, bq = cfg.batch, cfg.seq, cfg.a_heads, cfg.bq_a
    nq = S // bq
    return pl.pallas_call(
        functools.partial(_attn_a_prompt_kernel, bq=bq, chunk=cfg.chunk, scale=cfg.a_qk ** -0.5, lam_init=lam_init),
        out_shape=jax.ShapeDtypeStruct((B * S, cfg.a_width), BF16),
        grid=(B, H, nq),
        in_specs=[pl.BlockSpec((bq, HEAD_LANES), lambda b, h, i: (b * nq + i, h)),
                  pl.BlockSpec((S, HEAD_LANES), lambda b, h, i: (b, H + h)),
                  pl.BlockSpec((S, HEAD_LANES), lambda b, h, i: (b, 2 * H + h)),
                  pl.BlockSpec((1, 1, wt5.shape[-1]), lambda b, h, i: (h, 0, 0)),
                  pl.BlockSpec(dl.shape, lambda b, h, i: (0, 0)),
                  pl.BlockSpec((1, HEAD_LANES), lambda b, h, i: (0, 0))],
        out_specs=pl.BlockSpec((bq, HEAD_LANES), lambda b, h, i: (b * nq + i, h)),
        scratch_shapes=[pltpu.VMEM((S, HEAD_LANES), BF16), pltpu.VMEM((S, HEAD_LANES), BF16),
                        pltpu.VMEM((bq, 2 * bq), F32)],
        compiler_params=_cparams(3, VMEM_LIMIT_V7X), name="attn_a_prompt",
    )(z, z, z, wt5, dl, g.reshape(1, HEAD_LANES))


def _softmax_parts(parts):
    m = functools.reduce(jnp.maximum, [jnp.max(s, axis=-1, keepdims=True) for s in parts])
    ps = [jnp.exp(s - m) for s in parts]
    l = functools.reduce(jnp.add, [jnp.sum(p, axis=-1, keepdims=True) for p in ps])
    return [p / l for p in ps]


def _attn_a_sample_kernel(q_ref, kn_ref, vn_ref, ck_ref, cv_ref, bias_ref, dl_ref, g_ref, o_ref,
                          *, heads, past, scale, lam_init):
    lam = _diff_lambda(dl_ref, lam_init)
    for h in range(heads):
        hs = slice(h * HEAD_LANES, (h + 1) * HEAD_LANES)
        q1, q2 = _split_q(q_ref[:, hs], HEAD_LANES // 2)
        kp = ck_ref[0, :, h, :].astype(BF16)
        kn = kn_ref[:, hs].astype(BF16)
        bp = bias_ref[h, :, 0:past]
        bn = bias_ref[h, :, past:]
        p1 = _softmax_parts([_dot_nt(q1, kp) * scale + bp, _dot_nt(q1, kn) * scale + bn])
        p2 = _softmax_parts([_dot_nt(q2, kp) * scale + bp, _dot_nt(q2, kn) * scale + bn])
        o = (_dot((p1[0] - lam * p2[0]).astype(BF16), cv_ref[0, :, h, :].astype(BF16))
             + _dot((p1[1] - lam * p2[1]).astype(BF16), vn_ref[:, hs].astype(BF16)))
        o_ref[:, hs] = (_rms(o, g_ref[...]) * (1.0 - lam_init)).astype(o_ref.dtype)


def attn_a_sample(cfg, z, cache_k, cache_v, layer, bias, dl, g, lam_init):
    Bd, T, H, P = cfg.dec_batch, cfg.dec_seq, cfg.a_heads, cfg.past_len
    W = cfg.a_width
    r0 = cfg.rows_p // T
    new = lambda c: pl.BlockSpec((T, W), lambda b: (r0 + b, c))
    cache = pl.BlockSpec((None, 1, P, H, HEAD_LANES), lambda b: (layer, b, 0, 0, 0))
    return pl.pallas_call(
        functools.partial(_attn_a_sample_kernel, heads=H, past=P, scale=cfg.a_qk ** -0.5, lam_init=lam_init),
        out_shape=jax.ShapeDtypeStruct((Bd * T, W), BF16),
        grid=(Bd,),
        in_specs=[new(0), new(1), new(2), cache, cache,
                  pl.BlockSpec(bias.shape, lambda b: (0, 0, 0)),
                  pl.BlockSpec(dl.shape, lambda b: (0, 0)),
                  pl.BlockSpec((1, HEAD_LANES), lambda b: (0, 0))],
        out_specs=pl.BlockSpec((T, W), lambda b: (b, 0)),
        compiler_params=_cparams(1, VMEM_LIMIT_V7X), name="attn_a_sample",
    )(z, z, z, cache_k, cache_v, bias, dl, g.reshape(1, HEAD_LANES))


def _attn_b_prompt_kernel(q_ref, k0, k1, k2, v0, v1, v2, w_ref, o_ref, bias_s, *, bq, group, chunk, left, scale):
    i = pl.program_id(2)

    @pl.when(i == 0)
    def _():
        r = lax.broadcasted_iota(jnp.int32, (bq, 3 * bq), 0)
        c = lax.broadcasted_iota(jnp.int32, (bq, 3 * bq), 1)
        qc, kc = _chunk_of(r, chunk), _chunk_of(c - 2 * bq, chunk)
        band = jnp.logical_and(kc <= qc, kc >= qc - left)
        for g in range(group):
            bias_s[g] = jnp.where(band, _toeplitz_tile(w_ref[g], bq, 3 * bq), NEG)

    for g in range(group):
        hs = slice(g * HEAD_LANES, (g + 1) * HEAD_LANES)
        q = q_ref[:, hs].astype(BF16)
        parts = []
        for j, kr in enumerate((k0, k1, k2)):
            pen = jnp.where(i - 2 + j >= 0, 0.0, NEG)
            parts.append(_dot_nt(q, kr[:, hs].astype(BF16)) * scale + bias_s[g, :, j * bq:(j + 1) * bq] + pen)
        ps = _softmax_parts(parts)
        o = functools.reduce(jnp.add, [_dot(p.astype(BF16), vr[:, hs].astype(BF16))
                                       for p, vr in zip(ps, (v0, v1, v2))])
        o_ref[:, hs] = o.astype(o_ref.dtype)


def attn_b_prompt(cfg, z, wband, layer):
    B, S, H, bq, G = cfg.batch, cfg.seq, cfg.b_heads, cfg.bq_b, cfg.b_group
    nq = S // bq
    gw = G * HEAD_LANES
    c0 = 3 * cfg.a_width // gw
    ng = H // G

    def kv(col, j):
        return pl.BlockSpec((bq, gw), lambda b, hg, i: (b * nq + jnp.maximum(i - 2 + j, 0), c0 + col * ng + hg))

    return pl.pallas_call(
        functools.partial(_attn_b_prompt_kernel, bq=bq, group=G, chunk=cfg.chunk, left=cfg.b_left_chunks,
                          scale=HEAD_LANES ** -0.5),
        out_shape=jax.ShapeDtypeStruct((B * S, cfg.b_width), BF16),
        grid=(B, ng, nq),
        in_specs=[pl.BlockSpec((bq, gw), lambda b, hg, i: (b * nq + i, c0 + hg)),
                  kv(1, 0), kv(1, 1), kv(1, 2), kv(2, 0), kv(2, 1), kv(2, 2),
                  pl.BlockSpec((None, G, 1, wband.shape[-1]), lambda b, hg, i: (layer, hg, 0, 0))],
        out_specs=pl.BlockSpec((bq, gw), lambda b, hg, i: (b * nq + i, hg)),
        scratch_shapes=[pltpu.VMEM((G, bq, 3 * bq), F32)],
        compiler_params=_cparams(3, VMEM_LIMIT_V7X), name="attn_b_prompt",
    )(z, z, z, z, z, z, z, wband)


def _attn_b_sample_kernel(q_ref, kn_ref, vn_ref, ck_ref, cv_ref, bias_ref, o_ref, *, heads, past, scale):
    for h in range(heads):
        hs = slice(h * HEAD_LANES, (h + 1) * HEAD_LANES)
        q = q_ref[:, hs].astype(BF16)
        ps = _softmax_parts([_dot_nt(q, ck_ref[0, :, h, :].astype(BF16)) * scale + bias_ref[h, :, 0:past],
                             _dot_nt(q, kn_ref[:, hs].astype(BF16)) * scale + bias_ref[h, :, past:]])
        o = (_dot(ps[0].astype(BF16), cv_ref[0, :, h, :].astype(BF16))
             + _dot(ps[1].astype(BF16), vn_ref[:, hs].astype(BF16)))
        o_ref[:, hs] = o.astype(o_ref.dtype)


def attn_b_sample(cfg, z, cache_k, cache_v, layer, bias):
    Bd, T, H = cfg.dec_batch, cfg.dec_seq, cfg.b_heads
    Pc = cache_k.shape[2]
    W = cfg.b_width
    r0 = cfg.rows_p // T
    c0 = 3 * cfg.a_width // W
    new = lambda c: pl.BlockSpec((T, W), lambda b: (r0 + b, c0 + c))
    cache = pl.BlockSpec((None, 1, Pc, H, HEAD_LANES), lambda b: (layer, b, 0, 0, 0))
    return pl.pallas_call(
        functools.partial(_attn_b_sample_kernel, heads=H, past=Pc, scale=HEAD_LANES ** -0.5),
        out_shape=jax.ShapeDtypeStruct((Bd * T, W), BF16),
        grid=(Bd,),
        in_specs=[new(0), new(1), new(2), cache, cache, pl.BlockSpec(bias.shape, lambda b: (0, 0, 0))],
        out_specs=pl.BlockSpec((T, W), lambda b: (b, 0)),
        compiler_params=_cparams(1, VMEM_LIMIT_V7X), name="attn_b_sample",
    )(z, z, z, cache_k, cache_v, bias)


def _split3(x):
    hi = x.astype(BF16)
    r = x - hi.astype(F32)
    mid = r.astype(BF16)
    lo = (r - mid.astype(F32)).astype(BF16)
    return hi, mid, lo


def _hgrn_head(qz, fz, vv, lb, st, L):
    sig = jax.nn.sigmoid(fz)
    logf = jnp.log(lb + (1.0 - lb) * sig)
    kk = (1.0 - lb) * (1.0 - sig)
    qq = _silu(qz)
    row = lax.broadcasted_iota(jnp.int32, (L, 1), 0)
    col = lax.broadcasted_iota(jnp.int32, (1, L), 1)
    tril = (col <= row).astype(BF16)
    b = functools.reduce(jnp.add, [_dot(tril, part) for part in _split3(logf)])
    vb = vv.astype(BF16)

    o = _dot_nt((qq * jnp.exp(b)).astype(BF16), st.astype(BF16))
    b_last = b[L - 1:L, :]
    kd = kk * jnp.exp(b_last - b)
    st_new = jnp.exp(b_last) * st + _dot_tn(vb, kd.astype(BF16))

    if L > SUB:
        p_off = jnp.zeros((L, L), F32)
        m = SUB
        while m < L:
            is_q = (row // m) % 2 == 1
            ref = b[m - 1:m, :]
            for g in range(1, L // (2 * m)):
                ref = jnp.where(row >= g * 2 * m, b[g * 2 * m + m - 1:g * 2 * m + m, :], ref)
            d = b - ref
            x = (jnp.where(is_q, qq, kk) * jnp.exp(jnp.where(is_q, d, -d))).astype(BF16)
            pair = jnp.logical_and(is_q, col // m == row // m - 1)
            p_off = p_off + jnp.where(pair, _dot_nt(x, x), 0.0)
            m *= 2
        o = o + _dot(p_off.astype(BF16), vb)

    ones = jnp.ones((HEAD_LANES, HEAD_LANES), BF16)
    t_idx = lax.broadcasted_iota(jnp.int32, (SUB, 1), 0)
    diag = []
    for r0 in range(0, L, SUB):
        bj, qj, kj, vj = (a[r0:r0 + SUB, :] for a in (b, qq, kk, vv))
        w = [jnp.exp(jnp.where(t_idx >= s, bj - bj[s:s + 1, :], NEG)) * qj * kj[s:s + 1, :] for s in range(SUB)]
        r = _dot(jnp.concatenate(w, axis=0).astype(BF16), ones)
        diag.append(functools.reduce(jnp.add, [r[s * SUB:(s + 1) * SUB, :] * vj[s:s + 1, :] for s in range(SUB)]))
    o = o + (jnp.concatenate(diag, axis=0) if len(diag) > 1 else diag[0])
    return o, st_new


def _hgrn_kernel(q_ref, f_ref, i_ref, g_ref, lb_ref, ng_ref, s0_ref, o_ref, so_ref, st_ref, *, heads, L):
    c = pl.program_id(1)

    @pl.when(c == 0)
    def _():
        st_ref[...] = s0_ref[0]

    for h in range(heads):
        hs = slice(h * HEAD_LANES, (h + 1) * HEAD_LANES)
        o, st_new = _hgrn_head(q_ref[:, hs], f_ref[:, hs], i_ref[:, hs], lb_ref[:, hs], st_ref[h], L)
        st_ref[h] = st_new
        o_ref[:, hs] = (_rms(o, ng_ref[...]) * _silu(g_ref[:, hs])).astype(o_ref.dtype)

    @pl.when(c == pl.num_programs(1) - 1)
    def _():
        so_ref[0] = st_ref[...]


def hgrn2(cfg, z, lb, norm_g, s0t, *, nb, T, row0):
    H, W = cfg.c_heads, cfg.c_width
    L = min(cfg.chunk, T)
    nc = T // L
    c0 = (3 * cfg.a_width + 3 * cfg.b_width) // W
    r0 = row0 // L
    col = lambda c: pl.BlockSpec((L, W), lambda b, i: (r0 + b * nc + i, c0 + c))
    state = pl.BlockSpec((1, H, HEAD_LANES, HEAD_LANES), lambda b, i: (b, 0, 0, 0))
    return pl.pallas_call(
        functools.partial(_hgrn_kernel, heads=H, L=L),
        out_shape=[jax.ShapeDtypeStruct((nb * T, W), BF16),
                   jax.ShapeDtypeStruct((nb, H, HEAD_LANES, HEAD_LANES), F32)],
        grid=(nb, nc),
        in_specs=[col(0), col(1), col(2), col(3),
                  pl.BlockSpec((1, W), lambda b, i: (0, 0)),
                  pl.BlockSpec((1, HEAD_LANES), lambda b, i: (0, 0)),
                  state],
        out_specs=[pl.BlockSpec((L, W), lambda b, i: (b * nc + i, 0)), state],
        scratch_shapes=[pltpu.VMEM((H, HEAD_LANES, HEAD_LANES), F32)],
        compiler_params=_cparams(2, VMEM_LIMIT_V7X), name="hgrn2",
    )(z, z, z, z, lb.reshape(1, W), norm_g.reshape(1, HEAD_LANES), s0t)


def _t5_bucket(cfg, rel):
    half = cfg.t5_buckets // 2
    n = -rel
    ret = jnp.where(n < 0, half, 0)
    n = jnp.abs(n)
    max_exact = half // 2
    nf = jnp.maximum(n, 1).astype(F32)
    large = max_exact + (jnp.log(nf / max_exact) / math.log(cfg.t5_max_dist / max_exact)
                         * (half - max_exact)).astype(jnp.int32)
    large = jnp.minimum(large, half - 1)
    return ret + jnp.where(n < max_exact, n, large)


def _t5_tables(cfg, t5_table):
    bq = cfg.bq_a
    _, off = _toeplitz_offsets(bq, 2 * bq)
    near = t5_table[_t5_bucket(cfg, jnp.asarray(off - bq))].T
    far = t5_table[_t5_bucket(cfg, jnp.array([-2 * bq]))].T
    wt5 = ((near - far) * LOG2E)[:, None, :]
    sq = cfg.past_len + jnp.arange(cfg.dec_seq)
    sk = jnp.arange(cfg.past_len + cfg.dec_seq)
    sample = jnp.moveaxis(t5_table[_t5_bucket(cfg, sk[None, :] - sq[:, None])], -1, 0)
    return wt5.astype(F32), sample.astype(F32)


def _band_tables(cfg, rel_table, n_cache):
    bq = cfg.bq_b
    clipped = lambda dist: jnp.clip(dist, -cfg.b_max_dist, cfg.b_max_dist) + cfg.b_max_dist
    _, off = _toeplitz_offsets(bq, 3 * bq)
    wband = rel_table[..., clipped(jnp.asarray(2 * bq - off))][:, :, None, :]
    sq = cfg.past_len + jnp.arange(cfg.dec_seq)
    kbpos = jnp.concatenate([cfg.past_len - n_cache + jnp.arange(n_cache), sq])
    sample = rel_table[..., clipped(sq[:, None] - kbpos[None, :])]
    return wband.astype(F32), sample.astype(F32)


def _check(cfg):
    assert cfg.rows_s % 16 == 0 and cfg.seq % cfg.rows_s == 0
    assert cfg.rows % cfg.bm == 0 and cfg.bm % 16 == 0
    assert cfg.d_model % cfg.bn == 0 and (cfg.n_in - 3 * cfg.d_model) % cfg.bn == 0 and cfg.n_in % cfg.bn == 0
    assert cfg.n_in % cfg.bn_wide == 0 and cfg.d_model % cfg.bn_wide == 0
    assert (3 * cfg.a_width + 3 * cfg.b_width) % cfg.c_width == 0 and (3 * cfg.a_width) % cfg.b_width == 0
    assert cfg.seq % cfg.bq_a == 0 and cfg.bq_a % cfg.chunk == 0 and cfg.bq_a >= cfg.t5_max_dist
    assert cfg.seq % cfg.bq_b == 0 and cfg.bq_b % cfg.chunk == 0 and 2 * cfg.bq_b >= cfg.b_win
    assert cfg.b_heads % cfg.b_group == 0 and (3 * cfg.a_width) % (cfg.b_group * HEAD_LANES) == 0
    assert cfg.chunk & (cfg.chunk - 1) == 0
    assert cfg.seq % cfg.chunk == 0 and cfg.chunk % SUB == 0 and cfg.dec_seq % SUB == 0
    assert cfg.past_len % cfg.chunk == 0 and cfg.dec_seq <= cfg.chunk
    assert min(cfg.b_win, cfg.past_len) <= cfg.b_win and cfg.rows_p % cfg.dec_seq == 0


def step(cfg, x_prompt, x_sample, cache_a_k, cache_a_v, cache_b_k, cache_b_v, state_c, c_prompt, c_sample,
         w_ada, b_ada, w_in, diff_lambda, a_subln_g, t5_bias, b_rel_bias, c_lb_param, c_norm_g,
         w_branch_a, w_branch_b, w_branch_c, w_o, ln1_g, ln1_b, ln2_g, ln2_b, w_up, w_down):
    _check(cfg)
    D, B, S, Bd, T = cfg.d_model, cfg.batch, cfg.seq, cfg.dec_batch, cfg.dec_seq
    RP = cfg.rows_p
    n_ctx = B + Bd
    pad = (-n_ctx) % 8
    c_all = jnp.concatenate([c_prompt, c_sample, jnp.zeros((pad, D), F32)], axis=0)
    mod = adaln_all(cfg, c_all, w_ada, b_ada)

    lb_soft = jax.nn.softmax(c_lb_param.astype(F32), axis=0)
    lb_all = jnp.cumsum(lb_soft, axis=0) - lb_soft[0]
    wt5, a_sample = _t5_tables(cfg, t5_bias)
    n_cache = cache_b_k.shape[2]
    wband, b_sample = _band_tables(cfg, b_rel_bias, n_cache)
    zero_state = jnp.zeros((B, cfg.c_heads, HEAD_LANES, HEAD_LANES), F32)
    A, Bw = cfg.a_width, cfg.b_width
    n_b_rows = min(cfg.b_win, S)

    def heads(a, lead):
        return a.reshape(lead + (a.shape[-1] // HEAD_LANES, HEAD_LANES))

    x, h = rowwise(cfg, (x_prompt.reshape(RP, D), x_sample.reshape(Bd * T, D)), mod_next=mod[0], sc_col=1, sh_col=0)
    outs = [[] for _ in range(10)]
    for l in range(cfg.depth):
        lam_init = 0.8 - 0.6 * math.exp(-0.3 * l)
        z = matmul(cfg, h, w_in, l, bn=cfg.bn_wide, name="w_in")

        oa = jnp.concatenate([attn_a_prompt(cfg, z, wt5, diff_lambda[l], a_subln_g[l], lam_init),
                              attn_a_sample(cfg, z, cache_a_k, cache_a_v, l, a_sample, diff_lambda[l], a_subln_g[l],
                                            lam_init)], axis=0)
        ob = jnp.concatenate([attn_b_prompt(cfg, z, wband, l),
                              attn_b_sample(cfg, z, cache_b_k, cache_b_v, l, b_sample[l])], axis=0)
        oc_p, st_p = hgrn2(cfg, z, lb_all[l], c_norm_g[l], zero_state, nb=B, T=S, row0=0)
        oc_s, st_s = hgrn2(cfg, z, lb_all[l], c_norm_g[l], jnp.swapaxes(state_c[l], -1, -2), nb=Bd, T=T, row0=RP)
        oc = jnp.concatenate([oc_p, oc_s], axis=0)

        mixed = branch_mix(cfg, oa, ob, oc, z, w_branch_a, w_branch_b, w_branch_c, l)
        y = matmul(cfg, mixed, w_o, l, bn=cfg.bn_wide, name="w_o")
        x, h2 = rowwise(cfg, x, y, mod[l], ln1_g[l], ln1_b[l], mod[l], g_col=2, sc_col=4, sh_col=3)
        u = matmul(cfg, h2, w_up, l, bn=cfg.bn_wide, out_dtype=BF16, relu2=True, name="w_up")
        m = matmul(cfg, u, w_down, l, bn=cfg.bn_wide, bm=cfg.bm // 2, row_passes=2, bk=D, name="w_down")
        if l + 1 < cfg.depth:
            x, h = rowwise(cfg, x, m, mod[l], ln2_g[l], ln2_b[l], mod[l + 1], g_col=5, sc_col=1, sh_col=0)
        else:
            x_p, x_s = rowwise(cfg, x, m, mod[l], ln2_g[l], ln2_b[l], split_out=True, g_col=5)

        kb0, vb0 = 3 * A + Bw, 3 * A + 2 * Bw
        tail = lambda c0: jnp.stack([z[(b + 1) * S - n_b_rows:(b + 1) * S, c0:c0 + Bw] for b in range(B)])
        outs[0].append(heads(z[:RP, A:2 * A], (B, S)))
        outs[1].append(heads(z[:RP, 2 * A:3 * A], (B, S)))
        outs[2].append(heads(tail(kb0), (B, n_b_rows)))
        outs[3].append(heads(tail(vb0), (B, n_b_rows)))
        outs[4].append(jnp.swapaxes(st_p, -1, -2))
        outs[5].append(heads(z[RP:, A:2 * A], (Bd, T)))
        outs[6].append(heads(z[RP:, 2 * A:3 * A], (Bd, T)))
        outs[7].append(heads(z[RP:, kb0:kb0 + Bw], (Bd, T)))
        outs[8].append(heads(z[RP:, vb0:vb0 + Bw], (Bd, T)))
        outs[9].append(jnp.swapaxes(st_s, -1, -2))

    return (x_p.reshape(B, S, D), x_s.reshape(Bd, T, D)) + tuple(jnp.stack(o) for o in outs)


def kernel(x_prompt, x_sample, cache_a_k, cache_a_v, cache_b_k, cache_b_v, state_c, c_prompt, c_sample, w_ada, b_ada, w_in, diff_lambda, a_subln_g, t5_bias, b_rel_bias, c_lb_param, c_norm_g, w_branch_a, w_branch_b, w_branch_c, w_o, ln1_g, ln1_b, ln2_g, ln2_b, w_up, w_down):
    return step(Cfg(), x_prompt, x_sample, cache_a_k, cache_a_v, cache_b_k, cache_b_v, state_c, c_prompt, c_sample,
                w_ada, b_ada, w_in, diff_lambda, a_subln_g, t5_bias, b_rel_bias, c_lb_param, c_norm_g,
                w_branch_a, w_branch_b, w_branch_c, w_o, ln1_g, ln1_b, ln2_g, ln2_b, w_up, w_down)
```

```python
import functools
import math
from typing import NamedTuple

import jax
import jax.numpy as jnp
import numpy as np
from jax import lax
from jax.experimental import pallas as pl
from jax.experimental.pallas import tpu as pltpu

F32 = jnp.float32
BF16 = jnp.bfloat16
NEG = -1e30
LOG2E = math.log2(math.e)
LN_EPS = 1e-5
RMS_EPS = 1e-6
HEAD_LANES = 128
SUB = 16
VMEM_LIMIT_V7X = 56 * 1024 * 1024


class Cfg(NamedTuple):
    d_model: int = 4096
    batch: int = 2
    seq: int = 4096
    depth: int = 4
    dec_batch: int = 8
    dec_seq: int = 16
    past_len: int = 1024
    chunk: int = 64
    a_heads: int = 12
    a_qk: int = 64
    b_heads: int = 12
    b_left_chunks: int = 8
    b_max_dist: int = 128
    c_heads: int = 8
    t5_buckets: int = 32
    t5_max_dist: int = 128
    bm: int = 640
    bn: int = 512
    bn_wide: int = 1024
    bq_a: int = 512
    bq_b: int = 256
    b_group: int = 4

    @property
    def a_width(self): return self.a_heads * HEAD_LANES
    @property
    def b_width(self): return self.b_heads * HEAD_LANES
    @property
    def c_width(self): return self.c_heads * HEAD_LANES
    @property
    def d_ff(self): return 4 * self.d_model
    @property
    def b_win(self): return self.b_left_chunks * self.chunk
    @property
    def n_in(self): return 3 * self.a_width + 3 * self.b_width + 4 * self.c_width + 3 * self.d_model
    @property
    def rows_p(self): return self.batch * self.seq
    @property
    def rows_s(self): return self.dec_batch * self.dec_seq
    @property
    def rows(self): return self.rows_p + self.rows_s
    @property
    def alpha(self): return (2 * self.depth) ** 0.25


def _cparams(n_axes, vmem=None):
    return pltpu.CompilerParams(dimension_semantics=("arbitrary",) * n_axes, vmem_limit_bytes=vmem)


def _dot(a, b):
    return jnp.dot(a, b, preferred_element_type=F32)


def _dot_nt(a, b):
    return lax.dot_general(a, b, (((1,), (1,)), ((), ())), preferred_element_type=F32)


def _dot_tn(a, b):
    return lax.dot_general(a, b, (((0,), (0,)), ((), ())), preferred_element_type=F32)


def _silu(x):
    return x * jax.nn.sigmoid(x)


def _cast_rows(dst_ref, src_ref, rows_per_step=512):
    n = src_ref.shape[0]
    step = min(rows_per_step, n)
    for r in range(0, n, step):
        dst_ref[r:r + step, :] = src_ref[r:r + step, :].astype(BF16)


def _adaln_kernel(c_ref, w_ref, b_ref, o_ref):
    s = _silu(c_ref[...]).astype(BF16)
    o_ref[0] = _dot(s, w_ref[0].astype(BF16)) + b_ref[0]


def adaln_all(cfg, c_all, w_ada, b_ada):
    R, D = c_all.shape
    L, _, N = w_ada.shape
    bn = cfg.bn
    return pl.pallas_call(
        _adaln_kernel,
        out_shape=jax.ShapeDtypeStruct((L, R, N), F32),
        grid=(L, N // bn),
        in_specs=[pl.BlockSpec((R, D), lambda l, j: (0, 0)),
                  pl.BlockSpec((1, D, bn), lambda l, j: (l, 0, j)),
                  pl.BlockSpec((1, 1, bn), lambda l, j: (l, 0, j))],
        out_specs=pl.BlockSpec((1, R, bn), lambda l, j: (l, 0, j)),
        compiler_params=_cparams(2, VMEM_LIMIT_V7X),
        name="adaln",
    )(c_all, w_ada, b_ada.reshape(L, 1, N))


def _rowwise_kernel(*refs, cfg, n_x, n_xo, do_ln, do_mod, g_col, sc_col, sh_col):
    refs = list(refs)
    x_refs = [refs.pop(0) for _ in range(n_x)]
    y_ref = refs.pop(0) if do_ln else None
    modg_ref = refs.pop(0) if do_ln else None
    lng_ref = refs.pop(0) if do_ln else None
    lnb_ref = refs.pop(0) if do_ln else None
    modn_ref = refs.pop(0) if do_mod else None
    xo_refs = [refs.pop(0) for _ in range(n_xo)]
    ho_ref = refs.pop(0) if do_mod else None
    D = cfg.d_model
    bm = cfg.rows_s
    n_prompt_blocks = cfg.rows_p // bm
    blocks_per_batch = cfg.seq // bm
    i = pl.program_id(0)

    def slab(x_ref, xo_ref, r0, nrows, brow):
        x = x_ref[r0:r0 + nrows, :]
        if do_ln:
            g = modg_ref[pl.ds(brow, 1), g_col * D:(g_col + 1) * D]
            u = cfg.alpha * x + (1.0 + g) * y_ref[r0:r0 + nrows, :]
            mu = jnp.mean(u, axis=-1, keepdims=True)
            var = jnp.mean(jnp.square(u - mu), axis=-1, keepdims=True)
            x = (u - mu) * lax.rsqrt(var + LN_EPS) * lng_ref[...] + lnb_ref[...]
        if xo_ref is not None:
            xo_ref[r0:r0 + nrows, :] = x
        if do_mod:
            sc = modn_ref[pl.ds(brow, 1), sc_col * D:(sc_col + 1) * D]
            sh = modn_ref[pl.ds(brow, 1), sh_col * D:(sh_col + 1) * D]
            ho_ref[r0:r0 + nrows, :] = (x * (1.0 + sc) + sh).astype(BF16)

    @pl.when(i < n_prompt_blocks)
    def _():
        slab(x_refs[0], xo_refs[0] if xo_refs else None, 0, bm, i // blocks_per_batch)

    @pl.when(i >= n_prompt_blocks)
    def _():
        for s in range(cfg.dec_batch):
            slab(x_refs[-1], xo_refs[-1] if xo_refs else None, s * cfg.dec_seq, cfg.dec_seq, cfg.batch + s)


def rowwise(cfg, x, y=None, mod_gate=None, ln_g=None, ln_b=None, mod_next=None, *, split_out=False,
            g_col=0, sc_col=0, sh_col=0):
    do_ln = y is not None
    do_mod = mod_next is not None
    xs = list(x) if isinstance(x, (tuple, list)) else [x]
    D = cfg.d_model
    M, bm = cfg.rows, cfg.rows_s
    n_prompt_blocks = cfg.rows_p // bm
    row = pl.BlockSpec((bm, D), lambda i: (i, 0))
    row_p = pl.BlockSpec((bm, D), lambda i: (jnp.minimum(i, n_prompt_blocks - 1), 0))
    row_s = pl.BlockSpec((bm, D), lambda i: (0, 0))
    full = lambda a: pl.BlockSpec(a.shape, lambda i: (0,) * a.ndim)
    args = list(xs)
    specs = [row] if len(xs) == 1 else [row_p, row_s]
    outs, out_specs = [], []
    if do_ln:
        ln_g = ln_g.reshape(1, D)
        ln_b = ln_b.reshape(1, D)
        args += [y, mod_gate, ln_g, ln_b]
        specs += [row, full(mod_gate), full(ln_g), full(ln_b)]
    if do_mod:
        args.append(mod_next)
        specs.append(full(mod_next))
    n_xo = 0
    if do_ln or len(xs) == 2:
        if split_out:
            outs += [jax.ShapeDtypeStruct((cfg.rows_p, D), F32), jax.ShapeDtypeStruct((cfg.rows_s, D), F32)]
            out_specs += [row_p, row_s]
        else:
            outs.append(jax.ShapeDtypeStruct((M, D), F32))
            out_specs.append(row)
        n_xo = len(outs)
    if do_mod:
        outs.append(jax.ShapeDtypeStruct((M, D), BF16))
        out_specs.append(row)
    return pl.pallas_call(
        functools.partial(_rowwise_kernel, cfg=cfg, n_x=len(xs), n_xo=n_xo, do_ln=do_ln, do_mod=do_mod,
                          g_col=g_col, sc_col=sc_col, sh_col=sh_col),
        out_shape=outs, grid=(M // bm,), in_specs=specs, out_specs=out_specs,
        compiler_params=_cparams(1, VMEM_LIMIT_V7X), name="rowwise",
    )(*args)


def _mm_kernel(x_ref, w_ref, o_ref, wbf_ref, *acc, nk, nch, ck, n_wsteps, relu2):
    t = pl.program_id(0)
    i = pl.program_id(1)

    @pl.when(jnp.logical_and(t < n_wsteps, i < nch))
    def _():
        wbf_ref[t % 2, pl.ds(pl.multiple_of(i * ck, ck), ck), :] = w_ref[...].astype(BF16)

    @pl.when(t >= 1)
    def _():
        s = t - 1
        d = _dot(x_ref[...], wbf_ref[s % 2])

        def finish(r):
            if relu2:
                r = jnp.square(jnp.maximum(r, 0.0))
            o_ref[...] = r.astype(o_ref.dtype)

        if nk == 1:
            finish(d)
        else:
            acc_ref = acc[0]
            k = s % nk

            @pl.when(k == 0)
            def _():
                acc_ref[i] = d

            @pl.when(jnp.logical_and(k > 0, k < nk - 1))
            def _():
                acc_ref[i] += d

            @pl.when(k == nk - 1)
            def _():
                finish(acc_ref[i] + d)


def matmul(cfg, x, w, layer, *, bn, out_dtype=F32, relu2=False, bk=None, name="matmul"):
    M, K = x.shape
    N = w.shape[-1]
    bm = cfg.bm
    bk = K if bk is None else bk
    assert N % bn == 0 and M % bm == 0 and K % bk == 0
    nk, nm = K // bk, M // bm
    nch = max(c for c in (1, 2, 4, 8) if c <= nm and bk % (16 * c) == 0)
    ck = bk // nch
    S = (N // bn) * nk

    def x_map(t, i):
        return (jnp.where(t == 0, 0, i), jnp.maximum(t - 1, 0) % nk)

    def w_map(t, i):
        tq = jnp.minimum(t, S - 1)
        c = jnp.where(t < S, jnp.minimum(i, nch - 1), nch - 1)
        return (layer, (tq % nk) * nch + c, tq // nk)

    def o_map(t, i):
        s = jnp.maximum(t - 1, 0)
        writes = jnp.logical_and(t >= 1, s % nk == nk - 1)
        return (jnp.where(writes, i, 0), s // nk)

    scratch = [pltpu.VMEM((2, bk, bn), BF16)]
    if nk > 1:
        scratch.append(pltpu.VMEM((nm, bm, bn), F32))
    return pl.pallas_call(
        functools.partial(_mm_kernel, nk=nk, nch=nch, ck=ck, n_wsteps=S, relu2=relu2),
        out_shape=jax.ShapeDtypeStruct((M, N), out_dtype),
        grid=(S + 1, nm),
        in_specs=[pl.BlockSpec((bm, bk), x_map), pl.BlockSpec((None, ck, bn), w_map)],
        out_specs=pl.BlockSpec((bm, bn), o_map),
        scratch_shapes=scratch,
        compiler_params=_cparams(2, VMEM_LIMIT_V7X), name=name,
    )(x, w)


def _mix_kernel(oa_ref, ob_ref, oc_ref, ga_ref, gb_ref, gc_ref, wa_ref, wb_ref, wc_ref, o_ref, wa_s, wb_s, wc_s):
    @pl.when(pl.program_id(1) == 0)
    def _():
        _cast_rows(wa_s, wa_ref)
        _cast_rows(wb_s, wb_ref)
        _cast_rows(wc_s, wc_ref)

    mixed = (jax.nn.sigmoid(ga_ref[...]) * _dot(oa_ref[...], wa_s[...])
             + jax.nn.sigmoid(gb_ref[...]) * _dot(ob_ref[...], wb_s[...])
             + jax.nn.sigmoid(gc_ref[...]) * _dot(oc_ref[...], wc_s[...]))
    o_ref[...] = mixed.astype(o_ref.dtype)


def branch_mix(cfg, oa, ob, oc, z, wa, wb, wc, layer):
    M = oa.shape[0]
    D = cfg.d_model
    bm, bn = cfg.bm, cfg.bn
    g0 = (3 * cfg.a_width + 3 * cfg.b_width + 4 * cfg.c_width) // bn
    gstep = D // bn
    act = lambda w: pl.BlockSpec((bm, w), lambda j, i: (i, 0))
    gate = lambda t: pl.BlockSpec((bm, bn), lambda j, i: (i, g0 + t * gstep + j))
    wgt = lambda w: pl.BlockSpec((None, w, bn), lambda j, i: (layer, 0, j))
    return pl.pallas_call(
        _mix_kernel,
        out_shape=jax.ShapeDtypeStruct((M, D), BF16),
        grid=(D // bn, M // bm),
        in_specs=[act(cfg.a_width), act(cfg.b_width), act(cfg.c_width), gate(0), gate(1), gate(2),
                  wgt(cfg.a_width), wgt(cfg.b_width), wgt(cfg.c_width)],
        out_specs=pl.BlockSpec((bm, bn), lambda j, i: (i, j)),
        scratch_shapes=[pltpu.VMEM((cfg.a_width, bn), BF16), pltpu.VMEM((cfg.b_width, bn), BF16),
                        pltpu.VMEM((cfg.c_width, bn), BF16)],
        compiler_params=_cparams(2, VMEM_LIMIT_V7X), name="branch_mix",
    )(oa, ob, oc, z, z, z, wa, wb, wc)


def _diff_lambda(dl_ref, lam_init):
    dl = dl_ref[...]
    s1 = jnp.sum(dl[0:1] * dl[1:2], axis=-1, keepdims=True)
    s2 = jnp.sum(dl[2:3] * dl[3:4], axis=-1, keepdims=True)
    return jnp.exp(s1) - jnp.exp(s2) + lam_init


def _split_q(q, half):
    lane = lax.broadcasted_iota(jnp.int32, q.shape, 1)
    return (jnp.where(lane < half, q, 0.0).astype(BF16), jnp.where(lane >= half, q, 0.0).astype(BF16))


def _rms(o, g):
    return o * lax.rsqrt(jnp.mean(jnp.square(o), axis=-1, keepdims=True) + RMS_EPS) * g


def _flash_step(s, v, st):
    m, l, acc = st
    m_new = jnp.maximum(m, jnp.max(s, axis=-1, keepdims=True))
    a = jnp.exp2(m - m_new)
    p = jnp.exp2(s - m_new)
    return (m_new, a * l + jnp.sum(p, axis=-1, keepdims=True), a * acc + _dot(p.astype(BF16), v))


def _chunk_of(pos, chunk):
    return lax.shift_right_arithmetic(pos, int(math.log2(chunk)))


def _attn_a_prompt_kernel(*refs, chained, bq, chunk, scale, lam_init):
    q_ref, k_ref, v_ref, wt5_ref, dl_ref, g_ref = refs[:6]
    o_ref, ko_ref, vo_ref, kbf, vbf, bias_s = refs[8 if chained else 6:]
    qb = pl.program_id(2)

    @pl.when(qb == 0)
    def _():
        ko_ref[0, 0] = k_ref[...]
        vo_ref[0, 0] = v_ref[...]
        _cast_rows(kbf, k_ref)
        _cast_rows(vbf, v_ref)
        r = lax.broadcasted_iota(jnp.int32, (bq, 2 * bq), 0)
        c = lax.broadcasted_iota(jnp.int32, (bq, 2 * bq), 1)
        visible = _chunk_of(c - bq, chunk) <= _chunk_of(r, chunk)
        bias_s[...] = jnp.where(visible, _toeplitz_tile(wt5_ref[0], bq, 2 * bq), NEG)

    q1, q2 = _split_q(q_ref[...] * (scale * LOG2E), HEAD_LANES // 2)

    def block(koff, bias, st):
        k = kbf[pl.ds(koff, bq), :]
        v = vbf[pl.ds(koff, bq), :]
        s1, s2 = _dot_nt(q1, k), _dot_nt(q2, k)
        if bias is not None:
            s1, s2 = s1 + bias, s2 + bias
        return (_flash_step(s1, v, st[0]), _flash_step(s2, v, st[1]))

    init = (jnp.full((bq, 1), NEG, F32), jnp.zeros((bq, 1), F32), jnp.zeros((bq, HEAD_LANES), F32))
    st = lax.fori_loop(0, jnp.maximum(qb - 1, 0),
                       lambda kb, st: block(pl.multiple_of(kb * bq, bq), None, st), (init, init))
    prev_pen = jnp.where(qb > 0, 0.0, NEG)
    st = block(pl.multiple_of(jnp.maximum(qb - 1, 0) * bq, bq), bias_s[:, 0:bq] + prev_pen, st)
    st = block(pl.multiple_of(qb * bq, bq), bias_s[:, bq:2 * bq], st)
    (_, l1, a1), (_, l2, a2) = st
    lam = _diff_lambda(dl_ref, lam_init)
    o = a1 / l1 - lam * (a2 / l2)
    o_ref[...] = (_rms(o, g_ref[...]) * (1.0 - lam_init)).astype(o_ref.dtype)


def _toeplitz_tile(w_row, nq, nk):
    x = jnp.broadcast_to(w_row, (nq, w_row.shape[-1]))
    return pltpu.roll(x, 0, 1, stride=1, stride_axis=0)[:, :nk]


def _toeplitz_offsets(nq, nk):
    W = -(-(nq + nk - 1) // HEAD_LANES) * HEAD_LANES
    j = np.arange(W)
    return W, np.where(j < nk, j, j - W)


def attn_a_prompt(cfg, z, wt5, dl, g, lam_init, layer, kv_stack=None):
    B, S, H, bq = cfg.batch, cfg.seq, cfg.a_heads, cfg.bq_a
    nq = S // bq
    chained = kv_stack is not None
    cache = jax.ShapeDtypeStruct((cfg.depth, B, H, S, HEAD_LANES), F32)
    cache_spec = pl.BlockSpec((None, 1, 1, S, HEAD_LANES), lambda b, h, i: (layer, b, h, 0, 0))
    in_specs = [pl.BlockSpec((bq, HEAD_LANES), lambda b, h, i: (b * nq + i, h)),
                pl.BlockSpec((S, HEAD_LANES), lambda b, h, i: (b, H + h)),
                pl.BlockSpec((S, HEAD_LANES), lambda b, h, i: (b, 2 * H + h)),
                pl.BlockSpec((1, 1, wt5.shape[-1]), lambda b, h, i: (h, 0, 0)),
                pl.BlockSpec(dl.shape, lambda b, h, i: (0, 0)),
                pl.BlockSpec((1, HEAD_LANES), lambda b, h, i: (0, 0))]
    args = [z, z, z, wt5, dl, g.reshape(1, HEAD_LANES)]
    if chained:
        in_specs += [pl.BlockSpec(memory_space=pl.ANY)] * 2
        args += list(kv_stack)
    o, ks, vs = pl.pallas_call(
        functools.partial(_attn_a_prompt_kernel, chained=chained, bq=bq, chunk=cfg.chunk,
                          scale=cfg.a_qk ** -0.5, lam_init=lam_init),
        out_shape=[jax.ShapeDtypeStruct((cfg.rows, cfg.a_width), BF16), cache, cache],
        grid=(B, H, nq),
        in_specs=in_specs,
        out_specs=[pl.BlockSpec((bq, HEAD_LANES), lambda b, h, i: (b * nq + i, h)), cache_spec, cache_spec],
        scratch_shapes=[pltpu.VMEM((S, HEAD_LANES), BF16), pltpu.VMEM((S, HEAD_LANES), BF16),
                        pltpu.VMEM((bq, 2 * bq), F32)],
        input_output_aliases={6: 1, 7: 2} if chained else {},
        compiler_params=_cparams(3, VMEM_LIMIT_V7X), name="attn_a_prompt",
    )(*args)
    return o, (ks, vs)


def _softmax_parts(parts):
    m = functools.reduce(jnp.maximum, [jnp.max(s, axis=-1, keepdims=True) for s in parts])
    ps = [jnp.exp(s - m) for s in parts]
    l = functools.reduce(jnp.add, [jnp.sum(p, axis=-1, keepdims=True) for p in ps])
    return [p / l for p in ps]


def _attn_a_sample_kernel(q_ref, kn_ref, vn_ref, ck_ref, cv_ref, bias_ref, dl_ref, g_ref, o_rows_ref, o_ref,
                          *, heads, past, scale, lam_init):
    del o_rows_ref
    lam = _diff_lambda(dl_ref, lam_init)
    for h in range(heads):
        hs = slice(h * HEAD_LANES, (h + 1) * HEAD_LANES)
        q1, q2 = _split_q(q_ref[:, hs], HEAD_LANES // 2)
        kp = ck_ref[0, h].astype(BF16)
        kn = kn_ref[:, hs].astype(BF16)
        bp = bias_ref[h, :, 0:past]
        bn = bias_ref[h, :, past:]
        p1 = _softmax_parts([_dot_nt(q1, kp) * scale + bp, _dot_nt(q1, kn) * scale + bn])
        p2 = _softmax_parts([_dot_nt(q2, kp) * scale + bp, _dot_nt(q2, kn) * scale + bn])
        o = (_dot((p1[0] - lam * p2[0]).astype(BF16), cv_ref[0, h].astype(BF16))
             + _dot((p1[1] - lam * p2[1]).astype(BF16), vn_ref[:, hs].astype(BF16)))
        o_ref[:, hs] = (_rms(o, g_ref[...]) * (1.0 - lam_init)).astype(o_ref.dtype)


def attn_a_sample(cfg, z, cache_k, cache_v, layer, bias, dl, g, lam_init, o_rows):
    Bd, T, H, P = cfg.dec_batch, cfg.dec_seq, cfg.a_heads, cfg.past_len
    W = cfg.a_width
    r0 = cfg.rows_p // T
    new = lambda c: pl.BlockSpec((T, W), lambda b: (r0 + b, c))
    cache = pl.BlockSpec((None, 1, H, P, HEAD_LANES), lambda b: (layer, b, 0, 0, 0))
    return pl.pallas_call(
        functools.partial(_attn_a_sample_kernel, heads=H, past=P, scale=cfg.a_qk ** -0.5, lam_init=lam_init),
        out_shape=jax.ShapeDtypeStruct(o_rows.shape, o_rows.dtype),
        grid=(Bd,),
        in_specs=[new(0), new(1), new(2), cache, cache,
                  pl.BlockSpec(bias.shape, lambda b: (0, 0, 0)),
                  pl.BlockSpec(dl.shape, lambda b: (0, 0)),
                  pl.BlockSpec((1, HEAD_LANES), lambda b: (0, 0)),
                  pl.BlockSpec(memory_space=pl.ANY)],
        out_specs=pl.BlockSpec((T, W), lambda b: (r0 + b, 0)),
        input_output_aliases={8: 0},
        compiler_params=_cparams(1, VMEM_LIMIT_V7X), name="attn_a_sample",
    )(z, z, z, cache_k, cache_v, bias, dl, g.reshape(1, HEAD_LANES), o_rows)


def _attn_b_prompt_kernel(q_ref, k0, k1, k2, v0, v1, v2, w_ref, o_ref, bias_s, *, bq, group, chunk, left, scale):
    i = pl.program_id(2)

    @pl.when(i == 0)
    def _():
        r = lax.broadcasted_iota(jnp.int32, (bq, 3 * bq), 0)
        c = lax.broadcasted_iota(jnp.int32, (bq, 3 * bq), 1)
        qc, kc = _chunk_of(r, chunk), _chunk_of(c - 2 * bq, chunk)
        band = jnp.logical_and(kc <= qc, kc >= qc - left)
        for g in range(group):
            bias_s[g] = jnp.where(band, _toeplitz_tile(w_ref[g], bq, 3 * bq), NEG)

    for g in range(group):
        hs = slice(g * HEAD_LANES, (g + 1) * HEAD_LANES)
        q = q_ref[:, hs].astype(BF16)
        parts = []
        for j, kr in enumerate((k0, k1, k2)):
            pen = jnp.where(i - 2 + j >= 0, 0.0, NEG)
            parts.append(_dot_nt(q, kr[:, hs].astype(BF16)) * scale + bias_s[g, :, j * bq:(j + 1) * bq] + pen)
        ps = _softmax_parts(parts)
        o = functools.reduce(jnp.add, [_dot(p.astype(BF16), vr[:, hs].astype(BF16))
                                       for p, vr in zip(ps, (v0, v1, v2))])
        o_ref[:, hs] = o.astype(o_ref.dtype)


def attn_b_prompt(cfg, z, wband, layer):
    B, S, H, bq, G = cfg.batch, cfg.seq, cfg.b_heads, cfg.bq_b, cfg.b_group
    nq = S // bq
    gw = G * HEAD_LANES
    c0 = 3 * cfg.a_width // gw
    ng = H // G

    def kv(col, j):
        return pl.BlockSpec((bq, gw), lambda b, hg, i: (b * nq + jnp.maximum(i - 2 + j, 0), c0 + col * ng + hg))

    return pl.pallas_call(
        functools.partial(_attn_b_prompt_kernel, bq=bq, group=G, chunk=cfg.chunk, left=cfg.b_left_chunks,
                          scale=HEAD_LANES ** -0.5),
        out_shape=jax.ShapeDtypeStruct((cfg.rows, cfg.b_width), BF16),
        grid=(B, ng, nq),
        in_specs=[pl.BlockSpec((bq, gw), lambda b, hg, i: (b * nq + i, c0 + hg)),
                  kv(1, 0), kv(1, 1), kv(1, 2), kv(2, 0), kv(2, 1), kv(2, 2),
                  pl.BlockSpec((None, G, 1, wband.shape[-1]), lambda b, hg, i: (layer, hg, 0, 0))],
        out_specs=pl.BlockSpec((bq, gw), lambda b, hg, i: (b * nq + i, hg)),
        scratch_shapes=[pltpu.VMEM((G, bq, 3 * bq), F32)],
        compiler_params=_cparams(3, VMEM_LIMIT_V7X), name="attn_b_prompt",
    )(z, z, z, z, z, z, z, wband)


def _attn_b_sample_kernel(q_ref, kn_ref, vn_ref, ck_ref, cv_ref, bias_ref, o_rows_ref, o_ref, *, heads, past, scale):
    del o_rows_ref
    for h in range(heads):
        hs = slice(h * HEAD_LANES, (h + 1) * HEAD_LANES)
        q = q_ref[:, hs].astype(BF16)
        ps = _softmax_parts([_dot_nt(q, ck_ref[0, h].astype(BF16)) * scale + bias_ref[h, :, 0:past],
                             _dot_nt(q, kn_ref[:, hs].astype(BF16)) * scale + bias_ref[h, :, past:]])
        o = (_dot(ps[0].astype(BF16), cv_ref[0, h].astype(BF16))
             + _dot(ps[1].astype(BF16), vn_ref[:, hs].astype(BF16)))
        o_ref[:, hs] = o.astype(o_ref.dtype)


def attn_b_sample(cfg, z, cache_k, cache_v, layer, bias, o_rows):
    Bd, T, H = cfg.dec_batch, cfg.dec_seq, cfg.b_heads
    Pc = cache_k.shape[3]
    W = cfg.b_width
    r0 = cfg.rows_p // T
    c0 = 3 * cfg.a_width // W
    new = lambda c: pl.BlockSpec((T, W), lambda b: (r0 + b, c0 + c))
    cache = pl.BlockSpec((None, 1, H, Pc, HEAD_LANES), lambda b: (layer, b, 0, 0, 0))
    return pl.pallas_call(
        functools.partial(_attn_b_sample_kernel, heads=H, past=Pc, scale=HEAD_LANES ** -0.5),
        out_shape=jax.ShapeDtypeStruct(o_rows.shape, o_rows.dtype),
        grid=(Bd,),
        in_specs=[new(0), new(1), new(2), cache, cache, pl.BlockSpec(bias.shape, lambda b: (0, 0, 0)),
                  pl.BlockSpec(memory_space=pl.ANY)],
        out_specs=pl.BlockSpec((T, W), lambda b: (r0 + b, 0)),
        input_output_aliases={6: 0},
        compiler_params=_cparams(1, VMEM_LIMIT_V7X), name="attn_b_sample",
    )(z, z, z, cache_k, cache_v, bias, o_rows)


def _split3(x):
    hi = x.astype(BF16)
    r = x - hi.astype(F32)
    mid = r.astype(BF16)
    lo = (r - mid.astype(F32)).astype(BF16)
    return hi, mid, lo


def _hgrn_head(qz, fz, vv, lb, st, L):
    sig = jax.nn.sigmoid(fz)
    logf = jnp.log(lb + (1.0 - lb) * sig)
    kk = (1.0 - lb) * (1.0 - sig)
    qq = _silu(qz)
    row = lax.broadcasted_iota(jnp.int32, (L, 1), 0)
    col = lax.broadcasted_iota(jnp.int32, (1, L), 1)
    tril = (col <= row).astype(BF16)
    b = functools.reduce(jnp.add, [_dot(tril, part) for part in _split3(logf)])
    vb = vv.astype(BF16)

    o = _dot_nt((qq * jnp.exp(b)).astype(BF16), st.astype(BF16))
    b_last = b[L - 1:L, :]
    kd = kk * jnp.exp(b_last - b)
    st_new = jnp.exp(b_last) * st + _dot_tn(vb, kd.astype(BF16))

    if L > SUB:
        p_off = jnp.zeros((L, L), F32)
        m = SUB
        while m < L:
            is_q = (row // m) % 2 == 1
            ref = b[m - 1:m, :]
            for g in range(1, L // (2 * m)):
                ref = jnp.where(row >= g * 2 * m, b[g * 2 * m + m - 1:g * 2 * m + m, :], ref)
            d = b - ref
            x = (jnp.where(is_q, qq, kk) * jnp.exp(jnp.where(is_q, d, -d))).astype(BF16)
            pair = jnp.logical_and(is_q, col // m == row // m - 1)
            p_off = p_off + jnp.where(pair, _dot_nt(x, x), 0.0)
            m *= 2
        o = o + _dot(p_off.astype(BF16), vb)

    ones = jnp.ones((HEAD_LANES, HEAD_LANES), BF16)
    t_idx = lax.broadcasted_iota(jnp.int32, (SUB, 1), 0)
    diag = []
    for r0 in range(0, L, SUB):
        bj, qj, kj, vj = (a[r0:r0 + SUB, :] for a in (b, qq, kk, vv))
        w = [jnp.exp(jnp.where(t_idx >= s, bj - bj[s:s + 1, :], NEG)) * qj * kj[s:s + 1, :] for s in range(SUB)]
        r = _dot(jnp.concatenate(w, axis=0).astype(BF16), ones)
        diag.append(functools.reduce(jnp.add, [r[s * SUB:(s + 1) * SUB, :] * vj[s:s + 1, :] for s in range(SUB)]))
    o = o + (jnp.concatenate(diag, axis=0) if len(diag) > 1 else diag[0])
    return o, st_new


def _hgrn_kernel(*refs, chained, heads, L):
    q_ref, f_ref, i_ref, g_ref, lb_ref, ng_ref, s0_ref = refs[:7]
    o_ref, so_ref, st_ref = refs[8 if chained else 7:]
    c = pl.program_id(1)

    @pl.when(c == 0)
    def _():
        st_ref[...] = s0_ref[0]

    for h in range(heads):
        hs = slice(h * HEAD_LANES, (h + 1) * HEAD_LANES)
        o, st_new = _hgrn_head(q_ref[:, hs], f_ref[:, hs], i_ref[:, hs], lb_ref[:, hs], st_ref[h], L)
        st_ref[h] = st_new
        o_ref[:, hs] = (_rms(o, ng_ref[...]) * _silu(g_ref[:, hs])).astype(o_ref.dtype)

    @pl.when(c == pl.num_programs(1) - 1)
    def _():
        so_ref[0] = st_ref[...]


def hgrn2(cfg, z, lb, norm_g, s0t, *, nb, T, row0, o_rows=None):
    H, W = cfg.c_heads, cfg.c_width
    L = min(cfg.chunk, T)
    nc = T // L
    c0 = (3 * cfg.a_width + 3 * cfg.b_width) // W
    r0 = row0 // L
    chained = o_rows is not None
    col = lambda c: pl.BlockSpec((L, W), lambda b, i: (r0 + b * nc + i, c0 + c))
    state = pl.BlockSpec((1, H, HEAD_LANES, HEAD_LANES), lambda b, i: (b, 0, 0, 0))
    in_specs = [col(0), col(1), col(2), col(3),
                pl.BlockSpec((1, W), lambda b, i: (0, 0)),
                pl.BlockSpec((1, HEAD_LANES), lambda b, i: (0, 0)),
                state]
    args = [z, z, z, z, lb.reshape(1, W), norm_g.reshape(1, HEAD_LANES), s0t]
    if chained:
        in_specs.append(pl.BlockSpec(memory_space=pl.ANY))
        args.append(o_rows)
    return pl.pallas_call(
        functools.partial(_hgrn_kernel, chained=chained, heads=H, L=L),
        out_shape=[jax.ShapeDtypeStruct((cfg.rows, W), BF16),
                   jax.ShapeDtypeStruct((nb, H, HEAD_LANES, HEAD_LANES), F32)],
        grid=(nb, nc),
        in_specs=in_specs,
        out_specs=[pl.BlockSpec((L, W), lambda b, i: (r0 + b * nc + i, 0)), state],
        scratch_shapes=[pltpu.VMEM((H, HEAD_LANES, HEAD_LANES), F32)],
        input_output_aliases={7: 0} if chained else {},
        compiler_params=_cparams(2, VMEM_LIMIT_V7X), name="hgrn2",
    )(*args)


def _t5_bucket(cfg, rel):
    half = cfg.t5_buckets // 2
    n = -rel
    ret = jnp.where(n < 0, half, 0)
    n = jnp.abs(n)
    max_exact = half // 2
    nf = jnp.maximum(n, 1).astype(F32)
    large = max_exact + (jnp.log(nf / max_exact) / math.log(cfg.t5_max_dist / max_exact)
                         * (half - max_exact)).astype(jnp.int32)
    large = jnp.minimum(large, half - 1)
    return ret + jnp.where(n < max_exact, n, large)


def _t5_tables(cfg, t5_table):
    bq = cfg.bq_a
    _, off = _toeplitz_offsets(bq, 2 * bq)
    near = t5_table[_t5_bucket(cfg, jnp.asarray(off - bq))].T
    far = t5_table[_t5_bucket(cfg, jnp.array([-2 * bq]))].T
    wt5 = ((near - far) * LOG2E)[:, None, :]
    sq = cfg.past_len + jnp.arange(cfg.dec_seq)
    sk = jnp.arange(cfg.past_len + cfg.dec_seq)
    sample = jnp.moveaxis(t5_table[_t5_bucket(cfg, sk[None, :] - sq[:, None])], -1, 0)
    return wt5.astype(F32), sample.astype(F32)


def _band_tables(cfg, rel_table, n_cache):
    bq = cfg.bq_b
    clipped = lambda dist: jnp.clip(dist, -cfg.b_max_dist, cfg.b_max_dist) + cfg.b_max_dist
    _, off = _toeplitz_offsets(bq, 3 * bq)
    wband = rel_table[..., clipped(jnp.asarray(2 * bq - off))][:, :, None, :]
    sq = cfg.past_len + jnp.arange(cfg.dec_seq)
    kbpos = jnp.concatenate([cfg.past_len - n_cache + jnp.arange(n_cache), sq])
    sample = rel_table[..., clipped(sq[:, None] - kbpos[None, :])]
    return wband.astype(F32), sample.astype(F32)


def _check(cfg):
    assert cfg.rows_s % 16 == 0 and cfg.seq % cfg.rows_s == 0
    assert cfg.rows % cfg.bm == 0 and cfg.bm % 16 == 0
    assert cfg.d_model % cfg.bn == 0 and (cfg.n_in - 3 * cfg.d_model) % cfg.bn == 0 and cfg.n_in % cfg.bn == 0
    assert cfg.n_in % cfg.bn_wide == 0 and cfg.d_model % cfg.bn_wide == 0
    assert (3 * cfg.a_width + 3 * cfg.b_width) % cfg.c_width == 0 and (3 * cfg.a_width) % cfg.b_width == 0
    assert cfg.seq % cfg.bq_a == 0 and cfg.bq_a % cfg.chunk == 0 and cfg.bq_a >= cfg.t5_max_dist
    assert cfg.seq % cfg.bq_b == 0 and cfg.bq_b % cfg.chunk == 0 and 2 * cfg.bq_b >= cfg.b_win
    assert cfg.b_heads % cfg.b_group == 0 and (3 * cfg.a_width) % (cfg.b_group * HEAD_LANES) == 0
    assert cfg.chunk & (cfg.chunk - 1) == 0
    assert cfg.seq % cfg.chunk == 0 and cfg.chunk % SUB == 0 and cfg.dec_seq % SUB == 0
    assert cfg.past_len % cfg.chunk == 0 and cfg.dec_seq <= cfg.chunk
    assert min(cfg.b_win, cfg.past_len) <= cfg.b_win and cfg.rows_p % cfg.dec_seq == 0


def step(cfg, x_prompt, x_sample, cache_a_k, cache_a_v, cache_b_k, cache_b_v, state_c, c_prompt, c_sample,
         w_ada, b_ada, w_in, diff_lambda, a_subln_g, t5_bias, b_rel_bias, c_lb_param, c_norm_g,
         w_branch_a, w_branch_b, w_branch_c, w_o, ln1_g, ln1_b, ln2_g, ln2_b, w_up, w_down):
    _check(cfg)
    D, B, S, Bd, T = cfg.d_model, cfg.batch, cfg.seq, cfg.dec_batch, cfg.dec_seq
    RP = cfg.rows_p
    n_ctx = B + Bd
    pad = (-n_ctx) % 8
    c_all = jnp.concatenate([c_prompt, c_sample, jnp.zeros((pad, D), F32)], axis=0)
    mod = adaln_all(cfg, c_all, w_ada, b_ada)

    lb_soft = jax.nn.softmax(c_lb_param.astype(F32), axis=0)
    lb_all = jnp.cumsum(lb_soft, axis=0) - lb_soft[0]
    wt5, a_sample = _t5_tables(cfg, t5_bias)
    n_cache = cache_b_k.shape[2]
    wband, b_sample = _band_tables(cfg, b_rel_bias, n_cache)
    zero_state = jnp.zeros((B, cfg.c_heads, HEAD_LANES, HEAD_LANES), F32)
    A, Bw = cfg.a_width, cfg.b_width
    n_b_rows = min(cfg.b_win, S)
    cak, cav, cbk, cbv = (jnp.swapaxes(c, 2, 3) for c in (cache_a_k, cache_a_v, cache_b_k, cache_b_v))
    kv_stack = None

    def heads(a, lead):
        return a.reshape(lead + (a.shape[-1] // HEAD_LANES, HEAD_LANES))

    x, h = rowwise(cfg, (x_prompt.reshape(RP, D), x_sample.reshape(Bd * T, D)), mod_next=mod[0], sc_col=1, sh_col=0)
    outs = [[] for _ in range(8)]
    for l in range(cfg.depth):
        lam_init = 0.8 - 0.6 * math.exp(-0.3 * l)
        z = matmul(cfg, h, w_in, l, bn=cfg.bn_wide, name="w_in")

        oa, kv_stack = attn_a_prompt(cfg, z, wt5, diff_lambda[l], a_subln_g[l], lam_init, l, kv_stack)
        oa = attn_a_sample(cfg, z, cak, cav, l, a_sample, diff_lambda[l], a_subln_g[l], lam_init, oa)
        ob = attn_b_sample(cfg, z, cbk, cbv, l, b_sample[l], attn_b_prompt(cfg, z, wband, l))
        oc, st_p = hgrn2(cfg, z, lb_all[l], c_norm_g[l], zero_state, nb=B, T=S, row0=0)
        oc, st_s = hgrn2(cfg, z, lb_all[l], c_norm_g[l], jnp.swapaxes(state_c[l], -1, -2), nb=Bd, T=T, row0=RP,
                         o_rows=oc)

        mixed = branch_mix(cfg, oa, ob, oc, z, w_branch_a, w_branch_b, w_branch_c, l)
        y = matmul(cfg, mixed, w_o, l, bn=cfg.bn_wide, name="w_o")
        x, h2 = rowwise(cfg, x, y, mod[l], ln1_g[l], ln1_b[l], mod[l], g_col=2, sc_col=4, sh_col=3)
        u = matmul(cfg, h2, w_up, l, bn=cfg.bn_wide, out_dtype=BF16, relu2=True, name="w_up")
        m = matmul(cfg, u, w_down, l, bn=cfg.bn, bk=D, name="w_down")
        if l + 1 < cfg.depth:
            x, h = rowwise(cfg, x, m, mod[l], ln2_g[l], ln2_b[l], mod[l + 1], g_col=5, sc_col=1, sh_col=0)
        else:
            x_p, x_s = rowwise(cfg, x, m, mod[l], ln2_g[l], ln2_b[l], split_out=True, g_col=5)

        kb0, vb0 = 3 * A + Bw, 3 * A + 2 * Bw
        tail = lambda c0: jnp.stack([z[(b + 1) * S - n_b_rows:(b + 1) * S, c0:c0 + Bw] for b in range(B)])
        outs[0].append(heads(tail(kb0), (B, n_b_rows)))
        outs[1].append(heads(tail(vb0), (B, n_b_rows)))
        outs[2].append(jnp.swapaxes(st_p, -1, -2))
        outs[3].append(heads(z[RP:, A:2 * A], (Bd, T)))
        outs[4].append(heads(z[RP:, 2 * A:3 * A], (Bd, T)))
        outs[5].append(heads(z[RP:, kb0:kb0 + Bw], (Bd, T)))
        outs[6].append(heads(z[RP:, vb0:vb0 + Bw], (Bd, T)))
        outs[7].append(jnp.swapaxes(st_s, -1, -2))

    new_a_kv = tuple(jnp.swapaxes(c, 2, 3) for c in kv_stack)
    return (x_p.reshape(B, S, D), x_s.reshape(Bd, T, D)) + new_a_kv + tuple(jnp.stack(o) for o in outs)


def kernel(x_prompt, x_sample, cache_a_k, cache_a_v, cache_b_k, cache_b_v, state_c, c_prompt, c_sample, w_ada, b_ada, w_in, diff_lambda, a_subln_g, t5_bias, b_rel_bias, c_lb_param, c_norm_g, w_branch_a, w_branch_b, w_branch_c, w_o, ln1_g, ln1_b, ln2_g, ln2_b, w_up, w_down):
    return step(Cfg(), x_prompt, x_sample, cache_a_k, cache_a_v, cache_b_k, cache_b_v, state_c, c_prompt, c_sample,
                w_ada, b_ada, w_in, diff_lambda, a_subln_g, t5_bias, b_rel_bias, c_lb_param, c_norm_g,
                w_branch_a, w_branch_b, w_branch_c, w_o, ln1_g, ln1_b, ln2_g, ln2_b, w_up, w_down)
```

```python
import functools
import math
from typing import NamedTuple

import jax
import jax.numpy as jnp
import numpy as np
from jax import lax
from jax.experimental import pallas as pl
from jax.experimental.pallas import tpu as pltpu

F32 = jnp.float32
BF16 = jnp.bfloat16
NEG = -1e30
LOG2E = math.log2(math.e)
LN_EPS = 1e-5
RMS_EPS = 1e-6
HEAD_LANES = 128
SUB = 16
VMEM_LIMIT_V7X = 56 * 1024 * 1024


class Cfg(NamedTuple):
    d_model: int = 4096
    batch: int = 2
    seq: int = 4096
    depth: int = 4
    dec_batch: int = 8
    dec_seq: int = 16
    past_len: int = 1024
    chunk: int = 64
    a_heads: int = 12
    a_qk: int = 64
    b_heads: int = 12
    b_left_chunks: int = 8
    b_max_dist: int = 128
    c_heads: int = 8
    t5_buckets: int = 32
    t5_max_dist: int = 128
    bm: int = 1040
    bm_mix: int = 640
    bn: int = 512
    bn_wide: int = 1024
    bq_a: int = 512
    bq_b: int = 256
    b_group: int = 4

    @property
    def a_width(self): return self.a_heads * HEAD_LANES
    @property
    def b_width(self): return self.b_heads * HEAD_LANES
    @property
    def c_width(self): return self.c_heads * HEAD_LANES
    @property
    def d_ff(self): return 4 * self.d_model
    @property
    def b_win(self): return self.b_left_chunks * self.chunk
    @property
    def n_in(self): return 3 * self.a_width + 3 * self.b_width + 4 * self.c_width + 3 * self.d_model
    @property
    def rows_p(self): return self.batch * self.seq
    @property
    def rows_s(self): return self.dec_batch * self.dec_seq
    @property
    def rows(self): return self.rows_p + self.rows_s
    @property
    def alpha(self): return (2 * self.depth) ** 0.25


def _cparams(n_axes, vmem=None):
    return pltpu.CompilerParams(dimension_semantics=("arbitrary",) * n_axes, vmem_limit_bytes=vmem)


def _dot(a, b):
    return jnp.dot(a, b, preferred_element_type=F32)


def _dot_nt(a, b):
    return lax.dot_general(a, b, (((1,), (1,)), ((), ())), preferred_element_type=F32)


def _dot_tn(a, b):
    return lax.dot_general(a, b, (((0,), (0,)), ((), ())), preferred_element_type=F32)


def _silu(x):
    return x * jax.nn.sigmoid(x)


def _cast_rows(dst_ref, src_ref, rows_per_step=512):
    n = src_ref.shape[0]
    step = min(rows_per_step, n)
    for r in range(0, n, step):
        dst_ref[r:r + step, :] = src_ref[r:r + step, :].astype(BF16)


def _adaln_kernel(c_ref, w_ref, b_ref, o_ref):
    s = _silu(c_ref[...]).astype(BF16)
    o_ref[0] = _dot(s, w_ref[0].astype(BF16)) + b_ref[0]


def adaln_all(cfg, c_all, w_ada, b_ada):
    R, D = c_all.shape
    L, _, N = w_ada.shape
    bn = cfg.bn
    return pl.pallas_call(
        _adaln_kernel,
        out_shape=jax.ShapeDtypeStruct((L, R, N), F32),
        grid=(L, N // bn),
        in_specs=[pl.BlockSpec((R, D), lambda l, j: (0, 0)),
                  pl.BlockSpec((1, D, bn), lambda l, j: (l, 0, j)),
                  pl.BlockSpec((1, 1, bn), lambda l, j: (l, 0, j))],
        out_specs=pl.BlockSpec((1, R, bn), lambda l, j: (l, 0, j)),
        compiler_params=_cparams(2, VMEM_LIMIT_V7X),
        name="adaln",
    )(c_all, w_ada, b_ada.reshape(L, 1, N))


def _rowwise_kernel(*refs, cfg, n_x, n_xo, do_ln, do_mod, g_col, sc_col, sh_col):
    refs = list(refs)
    x_refs = [refs.pop(0) for _ in range(n_x)]
    y_ref = refs.pop(0) if do_ln else None
    modg_ref = refs.pop(0) if do_ln else None
    lng_ref = refs.pop(0) if do_ln else None
    lnb_ref = refs.pop(0) if do_ln else None
    modn_ref = refs.pop(0) if do_mod else None
    xo_refs = [refs.pop(0) for _ in range(n_xo)]
    ho_ref = refs.pop(0) if do_mod else None
    D = cfg.d_model
    bm = cfg.rows_s
    n_prompt_blocks = cfg.rows_p // bm
    blocks_per_batch = cfg.seq // bm
    i = pl.program_id(0)

    def slab(x_ref, xo_ref, r0, nrows, brow):
        x = x_ref[r0:r0 + nrows, :]
        if do_ln:
            g = modg_ref[pl.ds(brow, 1), g_col * D:(g_col + 1) * D]
            u = cfg.alpha * x + (1.0 + g) * y_ref[r0:r0 + nrows, :]
            mu = jnp.mean(u, axis=-1, keepdims=True)
            var = jnp.mean(jnp.square(u - mu), axis=-1, keepdims=True)
            x = (u - mu) * lax.rsqrt(var + LN_EPS) * lng_ref[...] + lnb_ref[...]
        if xo_ref is not None:
            xo_ref[r0:r0 + nrows, :] = x
        if do_mod:
            sc = modn_ref[pl.ds(brow, 1), sc_col * D:(sc_col + 1) * D]
            sh = modn_ref[pl.ds(brow, 1), sh_col * D:(sh_col + 1) * D]
            ho_ref[r0:r0 + nrows, :] = (x * (1.0 + sc) + sh).astype(BF16)

    @pl.when(i < n_prompt_blocks)
    def _():
        slab(x_refs[0], xo_refs[0] if xo_refs else None, 0, bm, i // blocks_per_batch)

    @pl.when(i >= n_prompt_blocks)
    def _():
        for s in range(cfg.dec_batch):
            slab(x_refs[-1], xo_refs[-1] if xo_refs else None, s * cfg.dec_seq, cfg.dec_seq, cfg.batch + s)


def rowwise(cfg, x, y=None, mod_gate=None, ln_g=None, ln_b=None, mod_next=None, *, split_out=False,
            g_col=0, sc_col=0, sh_col=0):
    do_ln = y is not None
    do_mod = mod_next is not None
    xs = list(x) if isinstance(x, (tuple, list)) else [x]
    D = cfg.d_model
    M, bm = cfg.rows, cfg.rows_s
    n_prompt_blocks = cfg.rows_p // bm
    row = pl.BlockSpec((bm, D), lambda i: (i, 0))
    row_p = pl.BlockSpec((bm, D), lambda i: (jnp.minimum(i, n_prompt_blocks - 1), 0))
    row_s = pl.BlockSpec((bm, D), lambda i: (0, 0))
    full = lambda a: pl.BlockSpec(a.shape, lambda i: (0,) * a.ndim)
    args = list(xs)
    specs = [row] if len(xs) == 1 else [row_p, row_s]
    outs, out_specs = [], []
    if do_ln:
        ln_g = ln_g.reshape(1, D)
        ln_b = ln_b.reshape(1, D)
        args += [y, mod_gate, ln_g, ln_b]
        specs += [row, full(mod_gate), full(ln_g), full(ln_b)]
    if do_mod:
        args.append(mod_next)
        specs.append(full(mod_next))
    n_xo = 0
    if do_ln or len(xs) == 2:
        if split_out:
            outs += [jax.ShapeDtypeStruct((cfg.rows_p, D), F32), jax.ShapeDtypeStruct((cfg.rows_s, D), F32)]
            out_specs += [row_p, row_s]
        else:
            outs.append(jax.ShapeDtypeStruct((M, D), F32))
            out_specs.append(row)
        n_xo = len(outs)
    if do_mod:
        outs.append(jax.ShapeDtypeStruct((M, D), BF16))
        out_specs.append(row)
    return pl.pallas_call(
        functools.partial(_rowwise_kernel, cfg=cfg, n_x=len(xs), n_xo=n_xo, do_ln=do_ln, do_mod=do_mod,
                          g_col=g_col, sc_col=sc_col, sh_col=sh_col),
        out_shape=outs, grid=(M // bm,), in_specs=specs, out_specs=out_specs,
        compiler_params=_cparams(1, VMEM_LIMIT_V7X), name="rowwise",
    )(*args)


def _mm_kernel(x_ref, w_ref, o_ref, wbf_ref, *acc, nk, nch, ck, n_wsteps, relu2):
    t = pl.program_id(0)
    i = pl.program_id(1)

    @pl.when(jnp.logical_and(t < n_wsteps, i < nch))
    def _():
        wbf_ref[t % 2, pl.ds(pl.multiple_of(i * ck, ck), ck), :] = w_ref[...].astype(BF16)

    @pl.when(t >= 1)
    def _():
        s = t - 1
        d = _dot(x_ref[...], wbf_ref[s % 2])

        def finish(r):
            if relu2:
                r = jnp.square(jnp.maximum(r, 0.0))
            o_ref[...] = r.astype(o_ref.dtype)

        if nk == 1:
            finish(d)
        else:
            acc_ref = acc[0]
            k = s % nk

            @pl.when(k == 0)
            def _():
                acc_ref[i] = d

            @pl.when(jnp.logical_and(k > 0, k < nk - 1))
            def _():
                acc_ref[i] += d

            @pl.when(k == nk - 1)
            def _():
                finish(acc_ref[i] + d)


def matmul(cfg, x, w, layer, *, bn, out_dtype=F32, relu2=False, bk=None, name="matmul"):
    M, K = x.shape
    N = w.shape[-1]
    bm = cfg.bm
    bk = K if bk is None else bk
    assert N % bn == 0 and M % bm == 0 and K % bk == 0
    nk, nm = K // bk, M // bm
    nch = max(c for c in (1, 2, 4, 8) if c <= nm and bk % (16 * c) == 0)
    ck = bk // nch
    S = (N // bn) * nk

    def x_map(t, i):
        return (jnp.where(t == 0, 0, i), jnp.maximum(t - 1, 0) % nk)

    def w_map(t, i):
        tq = jnp.minimum(t, S - 1)
        c = jnp.where(t < S, jnp.minimum(i, nch - 1), nch - 1)
        return (layer, (tq % nk) * nch + c, tq // nk)

    def o_map(t, i):
        s = jnp.maximum(t - 1, 0)
        writes = jnp.logical_and(t >= 1, s % nk == nk - 1)
        return (jnp.where(writes, i, 0), s // nk)

    scratch = [pltpu.VMEM((2, bk, bn), BF16)]
    if nk > 1:
        scratch.append(pltpu.VMEM((nm, bm, bn), F32))
    return pl.pallas_call(
        functools.partial(_mm_kernel, nk=nk, nch=nch, ck=ck, n_wsteps=S, relu2=relu2),
        out_shape=jax.ShapeDtypeStruct((M, N), out_dtype),
        grid=(S + 1, nm),
        in_specs=[pl.BlockSpec((bm, bk), x_map), pl.BlockSpec((None, ck, bn), w_map)],
        out_specs=pl.BlockSpec((bm, bn), o_map),
        scratch_shapes=scratch,
        compiler_params=_cparams(2, VMEM_LIMIT_V7X), name=name,
    )(x, w)


def _mix_kernel(oa_ref, ob_ref, oc_ref, ga_ref, gb_ref, gc_ref, wa_ref, wb_ref, wc_ref, o_ref, wa_s, wb_s, wc_s):
    @pl.when(pl.program_id(1) == 0)
    def _():
        _cast_rows(wa_s, wa_ref)
        _cast_rows(wb_s, wb_ref)
        _cast_rows(wc_s, wc_ref)

    mixed = (jax.nn.sigmoid(ga_ref[...]) * _dot(oa_ref[...], wa_s[...])
             + jax.nn.sigmoid(gb_ref[...]) * _dot(ob_ref[...], wb_s[...])
             + jax.nn.sigmoid(gc_ref[...]) * _dot(oc_ref[...], wc_s[...]))
    o_ref[...] = mixed.astype(o_ref.dtype)


def branch_mix(cfg, oa, ob, oc, z, wa, wb, wc, layer):
    M = oa.shape[0]
    D = cfg.d_model
    bm, bn = cfg.bm_mix, cfg.bn
    g0 = (3 * cfg.a_width + 3 * cfg.b_width + 4 * cfg.c_width) // bn
    gstep = D // bn
    act = lambda w: pl.BlockSpec((bm, w), lambda j, i: (i, 0))
    gate = lambda t: pl.BlockSpec((bm, bn), lambda j, i: (i, g0 + t * gstep + j))
    wgt = lambda w: pl.BlockSpec((None, w, bn), lambda j, i: (layer, 0, j))
    return pl.pallas_call(
        _mix_kernel,
        out_shape=jax.ShapeDtypeStruct((M, D), BF16),
        grid=(D // bn, M // bm),
        in_specs=[act(cfg.a_width), act(cfg.b_width), act(cfg.c_width), gate(0), gate(1), gate(2),
                  wgt(cfg.a_width), wgt(cfg.b_width), wgt(cfg.c_width)],
        out_specs=pl.BlockSpec((bm, bn), lambda j, i: (i, j)),
        scratch_shapes=[pltpu.VMEM((cfg.a_width, bn), BF16), pltpu.VMEM((cfg.b_width, bn), BF16),
                        pltpu.VMEM((cfg.c_width, bn), BF16)],
        compiler_params=_cparams(2, VMEM_LIMIT_V7X), name="branch_mix",
    )(oa, ob, oc, z, z, z, wa, wb, wc)


def _diff_lambda(dl_ref, lam_init):
    dl = dl_ref[...]
    s1 = jnp.sum(dl[0:1] * dl[1:2], axis=-1, keepdims=True)
    s2 = jnp.sum(dl[2:3] * dl[3:4], axis=-1, keepdims=True)
    return jnp.exp(s1) - jnp.exp(s2) + lam_init


def _split_q(q, half):
    lane = lax.broadcasted_iota(jnp.int32, q.shape, 1)
    return (jnp.where(lane < half, q, 0.0).astype(BF16), jnp.where(lane >= half, q, 0.0).astype(BF16))


def _rms(o, g):
    return o * lax.rsqrt(jnp.mean(jnp.square(o), axis=-1, keepdims=True) + RMS_EPS) * g


def _flash_step(s, v, st):
    m, l, acc = st
    m_new = jnp.maximum(m, jnp.max(s, axis=-1, keepdims=True))
    a = jnp.exp2(m - m_new)
    p = jnp.exp2(s - m_new)
    return (m_new, a * l + jnp.sum(p, axis=-1, keepdims=True), a * acc + _dot(p.astype(BF16), v))


def _chunk_of(pos, chunk):
    return lax.shift_right_arithmetic(pos, int(math.log2(chunk)))


def _attn_a_prompt_kernel(*refs, chained, bq, chunk, scale, lam_init):
    q_ref, k_ref, v_ref, wt5_ref, dl_ref, g_ref = refs[:6]
    o_ref, ko_ref, vo_ref, kbf, vbf, bias_s = refs[8 if chained else 6:]
    qb = pl.program_id(2)

    @pl.when(qb == 0)
    def _():
        ko_ref[0, 0] = k_ref[...]
        vo_ref[0, 0] = v_ref[...]
        _cast_rows(kbf, k_ref)
        _cast_rows(vbf, v_ref)
        r = lax.broadcasted_iota(jnp.int32, (bq, 2 * bq), 0)
        c = lax.broadcasted_iota(jnp.int32, (bq, 2 * bq), 1)
        visible = _chunk_of(c - bq, chunk) <= _chunk_of(r, chunk)
        bias_s[...] = jnp.where(visible, _toeplitz_tile(wt5_ref[0], bq, 2 * bq), NEG)

    q1, q2 = _split_q(q_ref[...] * (scale * LOG2E), HEAD_LANES // 2)

    def block(koff, bias, st):
        k = kbf[pl.ds(koff, bq), :]
        v = vbf[pl.ds(koff, bq), :]
        s1, s2 = _dot_nt(q1, k), _dot_nt(q2, k)
        if bias is not None:
            s1, s2 = s1 + bias, s2 + bias
        return (_flash_step(s1, v, st[0]), _flash_step(s2, v, st[1]))

    init = (jnp.full((bq, 1), NEG, F32), jnp.zeros((bq, 1), F32), jnp.zeros((bq, HEAD_LANES), F32))
    st = lax.fori_loop(0, jnp.maximum(qb - 1, 0),
                       lambda kb, st: block(pl.multiple_of(kb * bq, bq), None, st), (init, init))
    prev_pen = jnp.where(qb > 0, 0.0, NEG)
    st = block(pl.multiple_of(jnp.maximum(qb - 1, 0) * bq, bq), bias_s[:, 0:bq] + prev_pen, st)
    st = block(pl.multiple_of(qb * bq, bq), bias_s[:, bq:2 * bq], st)
    (_, l1, a1), (_, l2, a2) = st
    lam = _diff_lambda(dl_ref, lam_init)
    o = a1 / l1 - lam * (a2 / l2)
    o_ref[...] = (_rms(o, g_ref[...]) * (1.0 - lam_init)).astype(o_ref.dtype)


def _toeplitz_tile(w_row, nq, nk):
    x = jnp.broadcast_to(w_row, (nq, w_row.shape[-1]))
    return pltpu.roll(x, 0, 1, stride=1, stride_axis=0)[:, :nk]


def _toeplitz_offsets(nq, nk):
    W = -(-(nq + nk - 1) // HEAD_LANES) * HEAD_LANES
    j = np.arange(W)
    return W, np.where(j < nk, j, j - W)


def attn_a_prompt(cfg, z, wt5, dl, g, lam_init, layer, kv_stack=None):
    B, S, H, bq = cfg.batch, cfg.seq, cfg.a_heads, cfg.bq_a
    nq = S // bq
    chained = kv_stack is not None
    cache = jax.ShapeDtypeStruct((cfg.depth, B, H, S, HEAD_LANES), F32)
    cache_spec = pl.BlockSpec((None, 1, 1, S, HEAD_LANES), lambda b, h, i: (layer, b, h, 0, 0))
    in_specs = [pl.BlockSpec((bq, HEAD_LANES), lambda b, h, i: (b * nq + i, h)),
                pl.BlockSpec((S, HEAD_LANES), lambda b, h, i: (b, H + h)),
                pl.BlockSpec((S, HEAD_LANES), lambda b, h, i: (b, 2 * H + h)),
                pl.BlockSpec((1, 1, wt5.shape[-1]), lambda b, h, i: (h, 0, 0)),
                pl.BlockSpec(dl.shape, lambda b, h, i: (0, 0)),
                pl.BlockSpec((1, HEAD_LANES), lambda b, h, i: (0, 0))]
    args = [z, z, z, wt5, dl, g.reshape(1, HEAD_LANES)]
    if chained:
        in_specs += [pl.BlockSpec(memory_space=pl.ANY)] * 2
        args += list(kv_stack)
    o, ks, vs = pl.pallas_call(
        functools.partial(_attn_a_prompt_kernel, chained=chained, bq=bq, chunk=cfg.chunk,
                          scale=cfg.a_qk ** -0.5, lam_init=lam_init),
        out_shape=[jax.ShapeDtypeStruct((cfg.rows, cfg.a_width), BF16), cache, cache],
        grid=(B, H, nq),
        in_specs=in_specs,
        out_specs=[pl.BlockSpec((bq, HEAD_LANES), lambda b, h, i: (b * nq + i, h)), cache_spec, cache_spec],
        scratch_shapes=[pltpu.VMEM((S, HEAD_LANES), BF16), pltpu.VMEM((S, HEAD_LANES), BF16),
                        pltpu.VMEM((bq, 2 * bq), F32)],
        input_output_aliases={6: 1, 7: 2} if chained else {},
        compiler_params=_cparams(3, VMEM_LIMIT_V7X), name="attn_a_prompt",
    )(*args)
    return o, (ks, vs)


def _softmax_parts(parts):
    m = functools.reduce(jnp.maximum, [jnp.max(s, axis=-1, keepdims=True) for s in parts])
    ps = [jnp.exp(s - m) for s in parts]
    l = functools.reduce(jnp.add, [jnp.sum(p, axis=-1, keepdims=True) for p in ps])
    return [p / l for p in ps]


def _attn_a_sample_kernel(q_ref, kn_ref, vn_ref, ck_ref, cv_ref, bias_ref, dl_ref, g_ref, o_rows_ref, o_ref,
                          *, heads, past, scale, lam_init):
    del o_rows_ref
    lam = _diff_lambda(dl_ref, lam_init)
    for h in range(heads):
        hs = slice(h * HEAD_LANES, (h + 1) * HEAD_LANES)
        q1, q2 = _split_q(q_ref[:, hs], HEAD_LANES // 2)
        kp = ck_ref[0, h].astype(BF16)
        kn = kn_ref[:, hs].astype(BF16)
        bp = bias_ref[h, :, 0:past]
        bn = bias_ref[h, :, past:]
        p1 = _softmax_parts([_dot_nt(q1, kp) * scale + bp, _dot_nt(q1, kn) * scale + bn])
        p2 = _softmax_parts([_dot_nt(q2, kp) * scale + bp, _dot_nt(q2, kn) * scale + bn])
        o = (_dot((p1[0] - lam * p2[0]).astype(BF16), cv_ref[0, h].astype(BF16))
             + _dot((p1[1] - lam * p2[1]).astype(BF16), vn_ref[:, hs].astype(BF16)))
        o_ref[:, hs] = (_rms(o, g_ref[...]) * (1.0 - lam_init)).astype(o_ref.dtype)


def attn_a_sample(cfg, z, cache_k, cache_v, layer, bias, dl, g, lam_init, o_rows):
    Bd, T, H, P = cfg.dec_batch, cfg.dec_seq, cfg.a_heads, cfg.past_len
    W = cfg.a_width
    r0 = cfg.rows_p // T
    new = lambda c: pl.BlockSpec((T, W), lambda b: (r0 + b, c))
    cache = pl.BlockSpec((None, 1, H, P, HEAD_LANES), lambda b: (layer, b, 0, 0, 0))
    return pl.pallas_call(
        functools.partial(_attn_a_sample_kernel, heads=H, past=P, scale=cfg.a_qk ** -0.5, lam_init=lam_init),
        out_shape=jax.ShapeDtypeStruct(o_rows.shape, o_rows.dtype),
        grid=(Bd,),
        in_specs=[new(0), new(1), new(2), cache, cache,
                  pl.BlockSpec(bias.shape, lambda b: (0, 0, 0)),
                  pl.BlockSpec(dl.shape, lambda b: (0, 0)),
                  pl.BlockSpec((1, HEAD_LANES), lambda b: (0, 0)),
                  pl.BlockSpec(memory_space=pl.ANY)],
        out_specs=pl.BlockSpec((T, W), lambda b: (r0 + b, 0)),
        input_output_aliases={8: 0},
        compiler_params=_cparams(1, VMEM_LIMIT_V7X), name="attn_a_sample",
    )(z, z, z, cache_k, cache_v, bias, dl, g.reshape(1, HEAD_LANES), o_rows)


def _attn_b_prompt_kernel(q_ref, k0, k1, k2, v0, v1, v2, w_ref, o_ref, bias_s, *, bq, group, chunk, left, scale):
    i = pl.program_id(2)

    @pl.when(i == 0)
    def _():
        r = lax.broadcasted_iota(jnp.int32, (bq, 3 * bq), 0)
        c = lax.broadcasted_iota(jnp.int32, (bq, 3 * bq), 1)
        qc, kc = _chunk_of(r, chunk), _chunk_of(c - 2 * bq, chunk)
        band = jnp.logical_and(kc <= qc, kc >= qc - left)
        for g in range(group):
            bias_s[g] = jnp.where(band, _toeplitz_tile(w_ref[g], bq, 3 * bq), NEG)

    for g in range(group):
        hs = slice(g * HEAD_LANES, (g + 1) * HEAD_LANES)
        q = q_ref[:, hs].astype(BF16)
        parts = []
        for j, kr in enumerate((k0, k1, k2)):
            pen = jnp.where(i - 2 + j >= 0, 0.0, NEG)
            parts.append(_dot_nt(q, kr[:, hs].astype(BF16)) * scale + bias_s[g, :, j * bq:(j + 1) * bq] + pen)
        ps = _softmax_parts(parts)
        o = functools.reduce(jnp.add, [_dot(p.astype(BF16), vr[:, hs].astype(BF16))
                                       for p, vr in zip(ps, (v0, v1, v2))])
        o_ref[:, hs] = o.astype(o_ref.dtype)


def attn_b_prompt(cfg, z, wband, layer):
    B, S, H, bq, G = cfg.batch, cfg.seq, cfg.b_heads, cfg.bq_b, cfg.b_group
    nq = S // bq
    gw = G * HEAD_LANES
    c0 = 3 * cfg.a_width // gw
    ng = H // G

    def kv(col, j):
        return pl.BlockSpec((bq, gw), lambda b, hg, i: (b * nq + jnp.maximum(i - 2 + j, 0), c0 + col * ng + hg))

    return pl.pallas_call(
        functools.partial(_attn_b_prompt_kernel, bq=bq, group=G, chunk=cfg.chunk, left=cfg.b_left_chunks,
                          scale=HEAD_LANES ** -0.5),
        out_shape=jax.ShapeDtypeStruct((cfg.rows, cfg.b_width), BF16),
        grid=(B, ng, nq),
        in_specs=[pl.BlockSpec((bq, gw), lambda b, hg, i: (b * nq + i, c0 + hg)),
                  kv(1, 0), kv(1, 1), kv(1, 2), kv(2, 0), kv(2, 1), kv(2, 2),
                  pl.BlockSpec((None, G, 1, wband.shape[-1]), lambda b, hg, i: (layer, hg, 0, 0))],
        out_specs=pl.BlockSpec((bq, gw), lambda b, hg, i: (b * nq + i, hg)),
        scratch_shapes=[pltpu.VMEM((G, bq, 3 * bq), F32)],
        compiler_params=_cparams(3, VMEM_LIMIT_V7X), name="attn_b_prompt",
    )(z, z, z, z, z, z, z, wband)


def _attn_b_sample_kernel(q_ref, kn_ref, vn_ref, ck_ref, cv_ref, bias_ref, o_rows_ref, o_ref, *, heads, past, scale):
    del o_rows_ref
    for h in range(heads):
        hs = slice(h * HEAD_LANES, (h + 1) * HEAD_LANES)
        q = q_ref[:, hs].astype(BF16)
        ps = _softmax_parts([_dot_nt(q, ck_ref[0, h].astype(BF16)) * scale + bias_ref[h, :, 0:past],
                             _dot_nt(q, kn_ref[:, hs].astype(BF16)) * scale + bias_ref[h, :, past:]])
        o = (_dot(ps[0].astype(BF16), cv_ref[0, h].astype(BF16))
             + _dot(ps[1].astype(BF16), vn_ref[:, hs].astype(BF16)))
        o_ref[:, hs] = o.astype(o_ref.dtype)


def attn_b_sample(cfg, z, cache_k, cache_v, layer, bias, o_rows):
    Bd, T, H = cfg.dec_batch, cfg.dec_seq, cfg.b_heads
    Pc = cache_k.shape[3]
    W = cfg.b_width
    r0 = cfg.rows_p // T
    c0 = 3 * cfg.a_width // W
    new = lambda c: pl.BlockSpec((T, W), lambda b: (r0 + b, c0 + c))
    cache = pl.BlockSpec((None, 1, H, Pc, HEAD_LANES), lambda b: (layer, b, 0, 0, 0))
    return pl.pallas_call(
        functools.partial(_attn_b_sample_kernel, heads=H, past=Pc, scale=HEAD_LANES ** -0.5),
        out_shape=jax.ShapeDtypeStruct(o_rows.shape, o_rows.dtype),
        grid=(Bd,),
        in_specs=[new(0), new(1), new(2), cache, cache, pl.BlockSpec(bias.shape, lambda b: (0, 0, 0)),
                  pl.BlockSpec(memory_space=pl.ANY)],
        out_specs=pl.BlockSpec((T, W), lambda b: (r0 + b, 0)),
        input_output_aliases={6: 0},
        compiler_params=_cparams(1, VMEM_LIMIT_V7X), name="attn_b_sample",
    )(z, z, z, cache_k, cache_v, bias, o_rows)


def _split3(x):
    hi = x.astype(BF16)
    r = x - hi.astype(F32)
    mid = r.astype(BF16)
    lo = (r - mid.astype(F32)).astype(BF16)
    return hi, mid, lo


def _hgrn_head(qz, fz, vv, lb, st, L):
    sig = jax.nn.sigmoid(fz)
    logf = jnp.log(lb + (1.0 - lb) * sig)
    kk = (1.0 - lb) * (1.0 - sig)
    qq = _silu(qz)
    row = lax.broadcasted_iota(jnp.int32, (L, 1), 0)
    col = lax.broadcasted_iota(jnp.int32, (1, L), 1)
    tril = (col <= row).astype(BF16)
    b = functools.reduce(jnp.add, [_dot(tril, part) for part in _split3(logf)])
    vb = vv.astype(BF16)

    o = _dot_nt((qq * jnp.exp(b)).astype(BF16), st.astype(BF16))
    b_last = b[L - 1:L, :]
    kd = kk * jnp.exp(b_last - b)
    st_new = jnp.exp(b_last) * st + _dot_tn(vb, kd.astype(BF16))

    if L > SUB:
        p_off = jnp.zeros((L, L), F32)
        m = SUB
        while m < L:
            is_q = (row // m) % 2 == 1
            ref = b[m - 1:m, :]
            for g in range(1, L // (2 * m)):
                ref = jnp.where(row >= g * 2 * m, b[g * 2 * m + m - 1:g * 2 * m + m, :], ref)
            d = b - ref
            x = (jnp.where(is_q, qq, kk) * jnp.exp(jnp.where(is_q, d, -d))).astype(BF16)
            pair = jnp.logical_and(is_q, col // m == row // m - 1)
            p_off = p_off + jnp.where(pair, _dot_nt(x, x), 0.0)
            m *= 2
        o = o + _dot(p_off.astype(BF16), vb)

    ones = jnp.ones((HEAD_LANES, HEAD_LANES), BF16)
    t_idx = lax.broadcasted_iota(jnp.int32, (SUB, 1), 0)
    diag = []
    for r0 in range(0, L, SUB):
        bj, qj, kj, vj = (a[r0:r0 + SUB, :] for a in (b, qq, kk, vv))
        w = [jnp.exp(jnp.where(t_idx >= s, bj - bj[s:s + 1, :], NEG)) * qj * kj[s:s + 1, :] for s in range(SUB)]
        r = _dot(jnp.concatenate(w, axis=0).astype(BF16), ones)
        diag.append(functools.reduce(jnp.add, [r[s * SUB:(s + 1) * SUB, :] * vj[s:s + 1, :] for s in range(SUB)]))
    o = o + (jnp.concatenate(diag, axis=0) if len(diag) > 1 else diag[0])
    return o, st_new


def _hgrn_kernel(*refs, chained, heads, L):
    q_ref, f_ref, i_ref, g_ref, lb_ref, ng_ref, s0_ref = refs[:7]
    o_ref, so_ref, st_ref = refs[8 if chained else 7:]
    c = pl.program_id(1)

    @pl.when(c == 0)
    def _():
        st_ref[...] = s0_ref[0]

    for h in range(heads):
        hs = slice(h * HEAD_LANES, (h + 1) * HEAD_LANES)
        o, st_new = _hgrn_head(q_ref[:, hs], f_ref[:, hs], i_ref[:, hs], lb_ref[:, hs], st_ref[h], L)
        st_ref[h] = st_new
        o_ref[:, hs] = (_rms(o, ng_ref[...]) * _silu(g_ref[:, hs])).astype(o_ref.dtype)

    @pl.when(c == pl.num_programs(1) - 1)
    def _():
        so_ref[0] = st_ref[...]


def hgrn2(cfg, z, lb, norm_g, s0t, *, nb, T, row0, o_rows=None):
    H, W = cfg.c_heads, cfg.c_width
    L = min(cfg.chunk, T)
    nc = T // L
    c0 = (3 * cfg.a_width + 3 * cfg.b_width) // W
    r0 = row0 // L
    chained = o_rows is not None
    col = lambda c: pl.BlockSpec((L, W), lambda b, i: (r0 + b * nc + i, c0 + c))
    state = pl.BlockSpec((1, H, HEAD_LANES, HEAD_LANES), lambda b, i: (b, 0, 0, 0))
    in_specs = [col(0), col(1), col(2), col(3),
                pl.BlockSpec((1, W), lambda b, i: (0, 0)),
                pl.BlockSpec((1, HEAD_LANES), lambda b, i: (0, 0)),
                state]
    args = [z, z, z, z, lb.reshape(1, W), norm_g.reshape(1, HEAD_LANES), s0t]
    if chained:
        in_specs.append(pl.BlockSpec(memory_space=pl.ANY))
        args.append(o_rows)
    return pl.pallas_call(
        functools.partial(_hgrn_kernel, chained=chained, heads=H, L=L),
        out_shape=[jax.ShapeDtypeStruct((cfg.rows, W), BF16),
                   jax.ShapeDtypeStruct((nb, H, HEAD_LANES, HEAD_LANES), F32)],
        grid=(nb, nc),
        in_specs=in_specs,
        out_specs=[pl.BlockSpec((L, W), lambda b, i: (r0 + b * nc + i, 0)), state],
        scratch_shapes=[pltpu.VMEM((H, HEAD_LANES, HEAD_LANES), F32)],
        input_output_aliases={7: 0} if chained else {},
        compiler_params=_cparams(2, VMEM_LIMIT_V7X), name="hgrn2",
    )(*args)


def _t5_bucket(cfg, rel):
    half = cfg.t5_buckets // 2
    n = -rel
    ret = jnp.where(n < 0, half, 0)
    n = jnp.abs(n)
    max_exact = half // 2
    nf = jnp.maximum(n, 1).astype(F32)
    large = max_exact + (jnp.log(nf / max_exact) / math.log(cfg.t5_max_dist / max_exact)
                         * (half - max_exact)).astype(jnp.int32)
    large = jnp.minimum(large, half - 1)
    return ret + jnp.where(n < max_exact, n, large)


def _t5_tables(cfg, t5_table):
    bq = cfg.bq_a
    _, off = _toeplitz_offsets(bq, 2 * bq)
    near = t5_table[_t5_bucket(cfg, jnp.asarray(off - bq))].T
    far = t5_table[_t5_bucket(cfg, jnp.array([-2 * bq]))].T
    wt5 = ((near - far) * LOG2E)[:, None, :]
    sq = cfg.past_len + jnp.arange(cfg.dec_seq)
    sk = jnp.arange(cfg.past_len + cfg.dec_seq)
    sample = jnp.moveaxis(t5_table[_t5_bucket(cfg, sk[None, :] - sq[:, None])], -1, 0)
    return wt5.astype(F32), sample.astype(F32)


def _band_tables(cfg, rel_table, n_cache):
    bq = cfg.bq_b
    clipped = lambda dist: jnp.clip(dist, -cfg.b_max_dist, cfg.b_max_dist) + cfg.b_max_dist
    _, off = _toeplitz_offsets(bq, 3 * bq)
    wband = rel_table[..., clipped(jnp.asarray(2 * bq - off))][:, :, None, :]
    sq = cfg.past_len + jnp.arange(cfg.dec_seq)
    kbpos = jnp.concatenate([cfg.past_len - n_cache + jnp.arange(n_cache), sq])
    sample = rel_table[..., clipped(sq[:, None] - kbpos[None, :])]
    return wband.astype(F32), sample.astype(F32)


def _check(cfg):
    assert cfg.rows_s % 16 == 0 and cfg.seq % cfg.rows_s == 0
    assert cfg.rows % cfg.bm == 0 and cfg.bm % 16 == 0 and cfg.rows % cfg.bm_mix == 0 and cfg.bm_mix % 16 == 0
    assert cfg.d_model % cfg.bn == 0 and (cfg.n_in - 3 * cfg.d_model) % cfg.bn == 0 and cfg.n_in % cfg.bn == 0
    assert cfg.n_in % cfg.bn_wide == 0 and cfg.d_model % cfg.bn_wide == 0
    assert (3 * cfg.a_width + 3 * cfg.b_width) % cfg.c_width == 0 and (3 * cfg.a_width) % cfg.b_width == 0
    assert cfg.seq % cfg.bq_a == 0 and cfg.bq_a % cfg.chunk == 0 and cfg.bq_a >= cfg.t5_max_dist
    assert cfg.seq % cfg.bq_b == 0 and cfg.bq_b % cfg.chunk == 0 and 2 * cfg.bq_b >= cfg.b_win
    assert cfg.b_heads % cfg.b_group == 0 and (3 * cfg.a_width) % (cfg.b_group * HEAD_LANES) == 0
    assert cfg.chunk & (cfg.chunk - 1) == 0
    assert cfg.seq % cfg.chunk == 0 and cfg.chunk % SUB == 0 and cfg.dec_seq % SUB == 0
    assert cfg.past_len % cfg.chunk == 0 and cfg.dec_seq <= cfg.chunk
    assert min(cfg.b_win, cfg.past_len) <= cfg.b_win and cfg.rows_p % cfg.dec_seq == 0


def step(cfg, x_prompt, x_sample, cache_a_k, cache_a_v, cache_b_k, cache_b_v, state_c, c_prompt, c_sample,
         w_ada, b_ada, w_in, diff_lambda, a_subln_g, t5_bias, b_rel_bias, c_lb_param, c_norm_g,
         w_branch_a, w_branch_b, w_branch_c, w_o, ln1_g, ln1_b, ln2_g, ln2_b, w_up, w_down):
    _check(cfg)
    D, B, S, Bd, T = cfg.d_model, cfg.batch, cfg.seq, cfg.dec_batch, cfg.dec_seq
    RP = cfg.rows_p
    n_ctx = B + Bd
    pad = (-n_ctx) % 8
    c_all = jnp.concatenate([c_prompt, c_sample, jnp.zeros((pad, D), F32)], axis=0)
    mod = adaln_all(cfg, c_all, w_ada, b_ada)

    lb_soft = jax.nn.softmax(c_lb_param.astype(F32), axis=0)
    lb_all = jnp.cumsum(lb_soft, axis=0) - lb_soft[0]
    wt5, a_sample = _t5_tables(cfg, t5_bias)
    n_cache = cache_b_k.shape[2]
    wband, b_sample = _band_tables(cfg, b_rel_bias, n_cache)
    zero_state = jnp.zeros((B, cfg.c_heads, HEAD_LANES, HEAD_LANES), F32)
    A, Bw = cfg.a_width, cfg.b_width
    n_b_rows = min(cfg.b_win, S)
    cak, cav, cbk, cbv = (jnp.swapaxes(c, 2, 3) for c in (cache_a_k, cache_a_v, cache_b_k, cache_b_v))
    kv_stack = None

    def heads(a, lead):
        return a.reshape(lead + (a.shape[-1] // HEAD_LANES, HEAD_LANES))

    x, h = rowwise(cfg, (x_prompt.reshape(RP, D), x_sample.reshape(Bd * T, D)), mod_next=mod[0], sc_col=1, sh_col=0)
    outs = [[] for _ in range(8)]
    for l in range(cfg.depth):
        lam_init = 0.8 - 0.6 * math.exp(-0.3 * l)
        z = matmul(cfg, h, w_in, l, bn=cfg.bn_wide, name="w_in")

        oa, kv_stack = attn_a_prompt(cfg, z, wt5, diff_lambda[l], a_subln_g[l], lam_init, l, kv_stack)
        oa = attn_a_sample(cfg, z, cak, cav, l, a_sample, diff_lambda[l], a_subln_g[l], lam_init, oa)
        ob = attn_b_sample(cfg, z, cbk, cbv, l, b_sample[l], attn_b_prompt(cfg, z, wband, l))
        oc, st_p = hgrn2(cfg, z, lb_all[l], c_norm_g[l], zero_state, nb=B, T=S, row0=0)
        oc, st_s = hgrn2(cfg, z, lb_all[l], c_norm_g[l], jnp.swapaxes(state_c[l], -1, -2), nb=Bd, T=T, row0=RP,
                         o_rows=oc)

        mixed = branch_mix(cfg, oa, ob, oc, z, w_branch_a, w_branch_b, w_branch_c, l)
        y = matmul(cfg, mixed, w_o, l, bn=cfg.bn_wide, out_dtype=BF16, name="w_o")
        x, h2 = rowwise(cfg, x, y, mod[l], ln1_g[l], ln1_b[l], mod[l], g_col=2, sc_col=4, sh_col=3)
        u = matmul(cfg, h2, w_up, l, bn=cfg.bn_wide, out_dtype=BF16, relu2=True, name="w_up")
        m = matmul(cfg, u, w_down, l, bn=cfg.bn, bk=D, out_dtype=BF16, name="w_down")
        if l + 1 < cfg.depth:
            x, h = rowwise(cfg, x, m, mod[l], ln2_g[l], ln2_b[l], mod[l + 1], g_col=5, sc_col=1, sh_col=0)
        else:
            x_p, x_s = rowwise(cfg, x, m, mod[l], ln2_g[l], ln2_b[l], split_out=True, g_col=5)

        kb0, vb0 = 3 * A + Bw, 3 * A + 2 * Bw
        tail = lambda c0: jnp.stack([z[(b + 1) * S - n_b_rows:(b + 1) * S, c0:c0 + Bw] for b in range(B)])
        outs[0].append(heads(tail(kb0), (B, n_b_rows)))
        outs[1].append(heads(tail(vb0), (B, n_b_rows)))
        outs[2].append(jnp.swapaxes(st_p, -1, -2))
        outs[3].append(heads(z[RP:, A:2 * A], (Bd, T)))
        outs[4].append(heads(z[RP:, 2 * A:3 * A], (Bd, T)))
        outs[5].append(heads(z[RP:, kb0:kb0 + Bw], (Bd, T)))
        outs[6].append(heads(z[RP:, vb0:vb0 + Bw], (Bd, T)))
        outs[7].append(jnp.swapaxes(st_s, -1, -2))

    new_a_kv = tuple(jnp.swapaxes(c, 2, 3) for c in kv_stack)
    return (x_p.reshape(B, S, D), x_s.reshape(Bd, T, D)) + new_a_kv + tuple(jnp.stack(o) for o in outs)


def kernel(x_prompt, x_sample, cache_a_k, cache_a_v, cache_b_k, cache_b_v, state_c, c_prompt, c_sample, w_ada, b_ada, w_in, diff_lambda, a_subln_g, t5_bias, b_rel_bias, c_lb_param, c_norm_g, w_branch_a, w_branch_b, w_branch_c, w_o, ln1_g, ln1_b, ln2_g, ln2_b, w_up, w_down):
    return step(Cfg(), x_prompt, x_sample, cache_a_k, cache_a_v, cache_b_k, cache_b_v, state_c, c_prompt, c_sample,
                w_ada, b_ada, w_in, diff_lambda, a_subln_g, t5_bias, b_rel_bias, c_lb_param, c_norm_g,
                w_branch_a, w_branch_b, w_branch_c, w_o, ln1_g, ln1_b, ln2_g, ln2_b, w_up, w_down)
```

```python
import functools
import math
from typing import NamedTuple

import jax
import jax.numpy as jnp
import numpy as np
from jax import lax
from jax.experimental import pallas as pl
from jax.experimental.pallas import tpu as pltpu

F32 = jnp.float32
BF16 = jnp.bfloat16
NEG = -1e30
LOG2E = math.log2(math.e)
LN_EPS = 1e-5
RMS_EPS = 1e-6
HEAD_LANES = 128
SUB = 16
VMEM_LIMIT_V7X = 56 * 1024 * 1024


class Cfg(NamedTuple):
    d_model: int = 4096
    batch: int = 2
    seq: int = 4096
    depth: int = 4
    dec_batch: int = 8
    dec_seq: int = 16
    past_len: int = 1024
    chunk: int = 64
    a_heads: int = 12
    a_qk: int = 64
    b_heads: int = 12
    b_left_chunks: int = 8
    b_max_dist: int = 128
    c_heads: int = 8
    t5_buckets: int = 32
    t5_max_dist: int = 128
    bm: int = 1040
    bm_mix: int = 640
    bn: int = 512
    bn_wide: int = 1024
    bq_a: int = 512
    bq_b: int = 256
    b_group: int = 4

    @property
    def a_width(self): return self.a_heads * HEAD_LANES
    @property
    def b_width(self): return self.b_heads * HEAD_LANES
    @property
    def c_width(self): return self.c_heads * HEAD_LANES
    @property
    def d_ff(self): return 4 * self.d_model
    @property
    def b_win(self): return self.b_left_chunks * self.chunk
    @property
    def n_in(self): return 3 * self.a_width + 3 * self.b_width + 4 * self.c_width + 3 * self.d_model
    @property
    def rows_p(self): return self.batch * self.seq
    @property
    def rows_s(self): return self.dec_batch * self.dec_seq
    @property
    def rows(self): return self.rows_p + self.rows_s
    @property
    def alpha(self): return (2 * self.depth) ** 0.25


def _cparams(n_axes, vmem=None):
    return pltpu.CompilerParams(dimension_semantics=("arbitrary",) * n_axes, vmem_limit_bytes=vmem)


def _dot(a, b):
    return jnp.dot(a, b, preferred_element_type=F32)


def _dot_nt(a, b):
    return lax.dot_general(a, b, (((1,), (1,)), ((), ())), preferred_element_type=F32)


def _dot_tn(a, b):
    return lax.dot_general(a, b, (((0,), (0,)), ((), ())), preferred_element_type=F32)


def _silu(x):
    return x * jax.nn.sigmoid(x)


def _cast_rows(dst_ref, src_ref, rows_per_step=512):
    n = src_ref.shape[0]
    step = min(rows_per_step, n)
    for r in range(0, n, step):
        dst_ref[r:r + step, :] = src_ref[r:r + step, :].astype(BF16)


def _adaln_kernel(c_ref, w_ref, b_ref, o_ref):
    s = _silu(c_ref[...]).astype(BF16)
    o_ref[0] = _dot(s, w_ref[0].astype(BF16)) + b_ref[0]


def adaln_all(cfg, c_all, w_ada, b_ada):
    R, D = c_all.shape
    L, _, N = w_ada.shape
    bn = cfg.bn
    return pl.pallas_call(
        _adaln_kernel,
        out_shape=jax.ShapeDtypeStruct((L, R, N), F32),
        grid=(L, N // bn),
        in_specs=[pl.BlockSpec((R, D), lambda l, j: (0, 0)),
                  pl.BlockSpec((1, D, bn), lambda l, j: (l, 0, j)),
                  pl.BlockSpec((1, 1, bn), lambda l, j: (l, 0, j))],
        out_specs=pl.BlockSpec((1, R, bn), lambda l, j: (l, 0, j)),
        compiler_params=_cparams(2, VMEM_LIMIT_V7X),
        name="adaln",
    )(c_all, w_ada, b_ada.reshape(L, 1, N))


def _rowwise_kernel(*refs, cfg, n_x, n_xo, do_ln, do_mod, g_col, sc_col, sh_col):
    refs = list(refs)
    x_refs = [refs.pop(0) for _ in range(n_x)]
    y_ref = refs.pop(0) if do_ln else None
    modg_ref = refs.pop(0) if do_ln else None
    lng_ref = refs.pop(0) if do_ln else None
    lnb_ref = refs.pop(0) if do_ln else None
    modn_ref = refs.pop(0) if do_mod else None
    xo_refs = [refs.pop(0) for _ in range(n_xo)]
    ho_ref = refs.pop(0) if do_mod else None
    D = cfg.d_model
    bm = cfg.rows_s
    n_prompt_blocks = cfg.rows_p // bm
    blocks_per_batch = cfg.seq // bm
    i = pl.program_id(0)

    def slab(x_ref, xo_ref, r0, nrows, brow):
        x = x_ref[r0:r0 + nrows, :]
        if do_ln:
            g = modg_ref[pl.ds(brow, 1), g_col * D:(g_col + 1) * D]
            u = cfg.alpha * x + (1.0 + g) * y_ref[r0:r0 + nrows, :]
            mu = jnp.mean(u, axis=-1, keepdims=True)
            var = jnp.mean(jnp.square(u - mu), axis=-1, keepdims=True)
            x = (u - mu) * lax.rsqrt(var + LN_EPS) * lng_ref[...] + lnb_ref[...]
        if xo_ref is not None:
            xo_ref[r0:r0 + nrows, :] = x
        if do_mod:
            sc = modn_ref[pl.ds(brow, 1), sc_col * D:(sc_col + 1) * D]
            sh = modn_ref[pl.ds(brow, 1), sh_col * D:(sh_col + 1) * D]
            ho_ref[r0:r0 + nrows, :] = (x * (1.0 + sc) + sh).astype(BF16)

    @pl.when(i < n_prompt_blocks)
    def _():
        slab(x_refs[0], xo_refs[0] if xo_refs else None, 0, bm, i // blocks_per_batch)

    @pl.when(i >= n_prompt_blocks)
    def _():
        for s in range(cfg.dec_batch):
            slab(x_refs[-1], xo_refs[-1] if xo_refs else None, s * cfg.dec_seq, cfg.dec_seq, cfg.batch + s)


def rowwise(cfg, x, y=None, mod_gate=None, ln_g=None, ln_b=None, mod_next=None, *, split_out=False,
            g_col=0, sc_col=0, sh_col=0):
    do_ln = y is not None
    do_mod = mod_next is not None
    xs = list(x) if isinstance(x, (tuple, list)) else [x]
    D = cfg.d_model
    M, bm = cfg.rows, cfg.rows_s
    n_prompt_blocks = cfg.rows_p // bm
    row = pl.BlockSpec((bm, D), lambda i: (i, 0))
    row_p = pl.BlockSpec((bm, D), lambda i: (jnp.minimum(i, n_prompt_blocks - 1), 0))
    row_s = pl.BlockSpec((bm, D), lambda i: (0, 0))
    full = lambda a: pl.BlockSpec(a.shape, lambda i: (0,) * a.ndim)
    args = list(xs)
    specs = [row] if len(xs) == 1 else [row_p, row_s]
    outs, out_specs = [], []
    if do_ln:
        ln_g = ln_g.reshape(1, D)
        ln_b = ln_b.reshape(1, D)
        args += [y, mod_gate, ln_g, ln_b]
        specs += [row, full(mod_gate), full(ln_g), full(ln_b)]
    if do_mod:
        args.append(mod_next)
        specs.append(full(mod_next))
    n_xo = 0
    if do_ln or len(xs) == 2:
        if split_out:
            outs += [jax.ShapeDtypeStruct((cfg.rows_p, D), F32), jax.ShapeDtypeStruct((cfg.rows_s, D), F32)]
            out_specs += [row_p, row_s]
        else:
            outs.append(jax.ShapeDtypeStruct((M, D), F32))
            out_specs.append(row)
        n_xo = len(outs)
    if do_mod:
        outs.append(jax.ShapeDtypeStruct((M, D), BF16))
        out_specs.append(row)
    return pl.pallas_call(
        functools.partial(_rowwise_kernel, cfg=cfg, n_x=len(xs), n_xo=n_xo, do_ln=do_ln, do_mod=do_mod,
                          g_col=g_col, sc_col=sc_col, sh_col=sh_col),
        out_shape=outs, grid=(M // bm,), in_specs=specs, out_specs=out_specs,
        compiler_params=_cparams(1, VMEM_LIMIT_V7X), name="rowwise",
    )(*args)


def _mm_kernel(x_ref, w_ref, o_ref, wbf_ref, *acc, nk, nch, ck, n_wsteps, relu2):
    t = pl.program_id(0)
    i = pl.program_id(1)

    @pl.when(jnp.logical_and(t < n_wsteps, i < nch))
    def _():
        wbf_ref[t % 2, pl.ds(pl.multiple_of(i * ck, ck), ck), :] = w_ref[...].astype(BF16)

    @pl.when(t >= 1)
    def _():
        s = t - 1
        d = _dot(x_ref[...], wbf_ref[s % 2])

        def finish(r):
            if relu2:
                r = jnp.square(jnp.maximum(r, 0.0))
            o_ref[...] = r.astype(o_ref.dtype)

        if nk == 1:
            finish(d)
        else:
            acc_ref = acc[0]
            k = s % nk

            @pl.when(k == 0)
            def _():
                acc_ref[i] = d

            @pl.when(jnp.logical_and(k > 0, k < nk - 1))
            def _():
                acc_ref[i] += d

            @pl.when(k == nk - 1)
            def _():
                finish(acc_ref[i] + d)


def matmul(cfg, x, w, layer, *, bn, out_dtype=F32, relu2=False, bk=None, name="matmul"):
    M, K = x.shape
    N = w.shape[-1]
    bm = cfg.bm
    bk = K if bk is None else bk
    assert N % bn == 0 and M % bm == 0 and K % bk == 0
    nk, nm = K // bk, M // bm
    nch = max(c for c in (1, 2, 4, 8) if c <= nm and bk % (16 * c) == 0)
    ck = bk // nch
    S = (N // bn) * nk

    def x_map(t, i):
        return (jnp.where(t == 0, 0, i), jnp.maximum(t - 1, 0) % nk)

    def w_map(t, i):
        tq = jnp.minimum(t, S - 1)
        c = jnp.where(t < S, jnp.minimum(i, nch - 1), nch - 1)
        return (layer, (tq % nk) * nch + c, tq // nk)

    def o_map(t, i):
        s = jnp.maximum(t - 1, 0)
        writes = jnp.logical_and(t >= 1, s % nk == nk - 1)
        return (jnp.where(writes, i, 0), s // nk)

    scratch = [pltpu.VMEM((2, bk, bn), BF16)]
    if nk > 1:
        scratch.append(pltpu.VMEM((nm, bm, bn), F32))
    return pl.pallas_call(
        functools.partial(_mm_kernel, nk=nk, nch=nch, ck=ck, n_wsteps=S, relu2=relu2),
        out_shape=jax.ShapeDtypeStruct((M, N), out_dtype),
        grid=(S + 1, nm),
        in_specs=[pl.BlockSpec((bm, bk), x_map), pl.BlockSpec((None, ck, bn), w_map)],
        out_specs=pl.BlockSpec((bm, bn), o_map),
        scratch_shapes=scratch,
        compiler_params=_cparams(2, VMEM_LIMIT_V7X), name=name,
    )(x, w)


def _mix_kernel(oa_ref, ob_ref, oc_ref, ga_ref, gb_ref, gc_ref, wa_ref, wb_ref, wc_ref, o_ref, wa_s, wb_s, wc_s):
    @pl.when(pl.program_id(1) == 0)
    def _():
        _cast_rows(wa_s, wa_ref)
        _cast_rows(wb_s, wb_ref)
        _cast_rows(wc_s, wc_ref)

    mixed = (jax.nn.sigmoid(ga_ref[...]) * _dot(oa_ref[...], wa_s[...])
             + jax.nn.sigmoid(gb_ref[...]) * _dot(ob_ref[...], wb_s[...])
             + jax.nn.sigmoid(gc_ref[...]) * _dot(oc_ref[...], wc_s[...]))
    o_ref[...] = mixed.astype(o_ref.dtype)


def branch_mix(cfg, oa, ob, oc, z, wa, wb, wc, layer):
    M = oa.shape[0]
    D = cfg.d_model
    bm, bn = cfg.bm_mix, cfg.bn
    g0 = (3 * cfg.a_width + 3 * cfg.b_width + 4 * cfg.c_width) // bn
    gstep = D // bn
    act = lambda w: pl.BlockSpec((bm, w), lambda j, i: (i, 0))
    gate = lambda t: pl.BlockSpec((bm, bn), lambda j, i: (i, g0 + t * gstep + j))
    wgt = lambda w: pl.BlockSpec((None, w, bn), lambda j, i: (layer, 0, j))
    return pl.pallas_call(
        _mix_kernel,
        out_shape=jax.ShapeDtypeStruct((M, D), BF16),
        grid=(D // bn, M // bm),
        in_specs=[act(cfg.a_width), act(cfg.b_width), act(cfg.c_width), gate(0), gate(1), gate(2),
                  wgt(cfg.a_width), wgt(cfg.b_width), wgt(cfg.c_width)],
        out_specs=pl.BlockSpec((bm, bn), lambda j, i: (i, j)),
        scratch_shapes=[pltpu.VMEM((cfg.a_width, bn), BF16), pltpu.VMEM((cfg.b_width, bn), BF16),
                        pltpu.VMEM((cfg.c_width, bn), BF16)],
        compiler_params=_cparams(2, VMEM_LIMIT_V7X), name="branch_mix",
    )(oa, ob, oc, z, z, z, wa, wb, wc)


def _diff_lambda(dl_ref, lam_init):
    dl = dl_ref[...]
    s1 = jnp.sum(dl[0:1] * dl[1:2], axis=-1, keepdims=True)
    s2 = jnp.sum(dl[2:3] * dl[3:4], axis=-1, keepdims=True)
    return jnp.exp(s1) - jnp.exp(s2) + lam_init


def _split_q(q, half):
    lane = lax.broadcasted_iota(jnp.int32, q.shape, 1)
    return (jnp.where(lane < half, q, 0.0).astype(BF16), jnp.where(lane >= half, q, 0.0).astype(BF16))


def _rms(o, g):
    return o * lax.rsqrt(jnp.mean(jnp.square(o), axis=-1, keepdims=True) + RMS_EPS) * g


def _flash_step(s, v, st):
    m, l, acc = st
    m_new = jnp.maximum(m, jnp.max(s, axis=-1, keepdims=True))
    a = jnp.exp2(m - m_new)
    p = jnp.exp2(s - m_new)
    return (m_new, a * l + jnp.sum(p, axis=-1, keepdims=True), a * acc + _dot(p.astype(BF16), v))


def _chunk_of(pos, chunk):
    return lax.shift_right_arithmetic(pos, int(math.log2(chunk)))


def _attn_a_prompt_kernel(*refs, bq, chunk, scale, lam_init):
    q_ref, k_ref, v_ref, wt5_ref, dl_ref, g_ref = refs[:6]
    o_ref, ko_ref, vo_ref, kbf, vbf, bias_s = refs[9:]
    qb = pl.program_id(2)

    @pl.when(qb == 0)
    def _():
        ko_ref[0, 0] = k_ref[...]
        vo_ref[0, 0] = v_ref[...]
        _cast_rows(kbf, k_ref)
        _cast_rows(vbf, v_ref)
        r = lax.broadcasted_iota(jnp.int32, (bq, 2 * bq), 0)
        c = lax.broadcasted_iota(jnp.int32, (bq, 2 * bq), 1)
        visible = _chunk_of(c - bq, chunk) <= _chunk_of(r, chunk)
        bias_s[...] = jnp.where(visible, _toeplitz_tile(wt5_ref[0], bq, 2 * bq), NEG)

    q1, q2 = _split_q(q_ref[...] * (scale * LOG2E), HEAD_LANES // 2)

    def block(koff, bias, st):
        k = kbf[pl.ds(koff, bq), :]
        v = vbf[pl.ds(koff, bq), :]
        s1, s2 = _dot_nt(q1, k), _dot_nt(q2, k)
        if bias is not None:
            s1, s2 = s1 + bias, s2 + bias
        return (_flash_step(s1, v, st[0]), _flash_step(s2, v, st[1]))

    init = (jnp.full((bq, 1), NEG, F32), jnp.zeros((bq, 1), F32), jnp.zeros((bq, HEAD_LANES), F32))
    st = lax.fori_loop(0, jnp.maximum(qb - 1, 0),
                       lambda kb, st: block(pl.multiple_of(kb * bq, bq), None, st), (init, init))
    prev_pen = jnp.where(qb > 0, 0.0, NEG)
    st = block(pl.multiple_of(jnp.maximum(qb - 1, 0) * bq, bq), bias_s[:, 0:bq] + prev_pen, st)
    st = block(pl.multiple_of(qb * bq, bq), bias_s[:, bq:2 * bq], st)
    (_, l1, a1), (_, l2, a2) = st
    lam = _diff_lambda(dl_ref, lam_init)
    o = a1 / l1 - lam * (a2 / l2)
    o_ref[...] = (_rms(o, g_ref[...]) * (1.0 - lam_init)).astype(o_ref.dtype)


def _toeplitz_tile(w_row, nq, nk):
    x = jnp.broadcast_to(w_row, (nq, w_row.shape[-1]))
    return pltpu.roll(x, 0, 1, stride=1, stride_axis=0)[:, :nk]


def _toeplitz_offsets(nq, nk):
    W = -(-(nq + nk - 1) // HEAD_LANES) * HEAD_LANES
    j = np.arange(W)
    return W, np.where(j < nk, j, j - W)


def attn_a_prompt(cfg, z, wt5, dl, g, lam_init, layer, kv_stack, o_rows):
    B, S, H, bq = cfg.batch, cfg.seq, cfg.a_heads, cfg.bq_a
    nq = S // bq
    cache = jax.ShapeDtypeStruct((cfg.depth, B, H, S, HEAD_LANES), F32)
    cache_spec = pl.BlockSpec((None, 1, 1, S, HEAD_LANES), lambda b, h, i: (layer, b, h, 0, 0))
    in_specs = [pl.BlockSpec((bq, HEAD_LANES), lambda b, h, i: (b * nq + i, h)),
                pl.BlockSpec((S, HEAD_LANES), lambda b, h, i: (b, H + h)),
                pl.BlockSpec((S, HEAD_LANES), lambda b, h, i: (b, 2 * H + h)),
                pl.BlockSpec((1, 1, wt5.shape[-1]), lambda b, h, i: (h, 0, 0)),
                pl.BlockSpec(dl.shape, lambda b, h, i: (0, 0)),
                pl.BlockSpec((1, HEAD_LANES), lambda b, h, i: (0, 0))]
    in_specs += [pl.BlockSpec(memory_space=pl.ANY)] * 3
    args = [z, z, z, wt5, dl, g.reshape(1, HEAD_LANES), kv_stack[0], kv_stack[1], o_rows]
    o, ks, vs = pl.pallas_call(
        functools.partial(_attn_a_prompt_kernel, bq=bq, chunk=cfg.chunk, scale=cfg.a_qk ** -0.5, lam_init=lam_init),
        out_shape=[jax.ShapeDtypeStruct(o_rows.shape, o_rows.dtype), cache, cache],
        grid=(B, H, nq),
        in_specs=in_specs,
        out_specs=[pl.BlockSpec((bq, HEAD_LANES), lambda b, h, i: (b * nq + i, h)), cache_spec, cache_spec],
        scratch_shapes=[pltpu.VMEM((S, HEAD_LANES), BF16), pltpu.VMEM((S, HEAD_LANES), BF16),
                        pltpu.VMEM((bq, 2 * bq), F32)],
        input_output_aliases={6: 1, 7: 2, 8: 0},
        compiler_params=_cparams(3, VMEM_LIMIT_V7X), name="attn_a_prompt",
    )(*args)
    return o, (ks, vs)


def _softmax_parts(parts):
    m = functools.reduce(jnp.maximum, [jnp.max(s, axis=-1, keepdims=True) for s in parts])
    ps = [jnp.exp(s - m) for s in parts]
    l = functools.reduce(jnp.add, [jnp.sum(p, axis=-1, keepdims=True) for p in ps])
    return [p / l for p in ps]


def _attn_a_sample_kernel(q_ref, kn_ref, vn_ref, ck_ref, cv_ref, bias_ref, dl_ref, g_ref, o_rows_ref, o_ref,
                          *, heads, past, scale, lam_init):
    del o_rows_ref
    lam = _diff_lambda(dl_ref, lam_init)
    for h in range(heads):
        hs = slice(h * HEAD_LANES, (h + 1) * HEAD_LANES)
        q1, q2 = _split_q(q_ref[:, hs], HEAD_LANES // 2)
        kp = ck_ref[0, h].astype(BF16)
        kn = kn_ref[:, hs].astype(BF16)
        bp = bias_ref[h, :, 0:past]
        bn = bias_ref[h, :, past:]
        p1 = _softmax_parts([_dot_nt(q1, kp) * scale + bp, _dot_nt(q1, kn) * scale + bn])
        p2 = _softmax_parts([_dot_nt(q2, kp) * scale + bp, _dot_nt(q2, kn) * scale + bn])
        o = (_dot((p1[0] - lam * p2[0]).astype(BF16), cv_ref[0, h].astype(BF16))
             + _dot((p1[1] - lam * p2[1]).astype(BF16), vn_ref[:, hs].astype(BF16)))
        o_ref[:, hs] = (_rms(o, g_ref[...]) * (1.0 - lam_init)).astype(o_ref.dtype)


def attn_a_sample(cfg, z, cache_k, cache_v, layer, bias, dl, g, lam_init, o_rows):
    Bd, T, H, P = cfg.dec_batch, cfg.dec_seq, cfg.a_heads, cfg.past_len
    W = cfg.a_width
    r0 = cfg.rows_p // T
    new = lambda c: pl.BlockSpec((T, W), lambda b: (r0 + b, c))
    cache = pl.BlockSpec((None, 1, H, P, HEAD_LANES), lambda b: (layer, b, 0, 0, 0))
    return pl.pallas_call(
        functools.partial(_attn_a_sample_kernel, heads=H, past=P, scale=cfg.a_qk ** -0.5, lam_init=lam_init),
        out_shape=jax.ShapeDtypeStruct(o_rows.shape, o_rows.dtype),
        grid=(Bd,),
        in_specs=[new(0), new(1), new(2), cache, cache,
                  pl.BlockSpec(bias.shape, lambda b: (0, 0, 0)),
                  pl.BlockSpec(dl.shape, lambda b: (0, 0)),
                  pl.BlockSpec((1, HEAD_LANES), lambda b: (0, 0)),
                  pl.BlockSpec(memory_space=pl.ANY)],
        out_specs=pl.BlockSpec((T, W), lambda b: (r0 + b, 0)),
        input_output_aliases={8: 0},
        compiler_params=_cparams(1, VMEM_LIMIT_V7X), name="attn_a_sample",
    )(z, z, z, cache_k, cache_v, bias, dl, g.reshape(1, HEAD_LANES), o_rows)


def _attn_b_prompt_kernel(q_ref, k0, k1, k2, v0, v1, v2, w_ref, o_rows_ref, o_ref, bias_s,
                          *, bq, group, chunk, left, scale):
    del o_rows_ref
    i = pl.program_id(2)

    @pl.when(i == 0)
    def _():
        r = lax.broadcasted_iota(jnp.int32, (bq, 3 * bq), 0)
        c = lax.broadcasted_iota(jnp.int32, (bq, 3 * bq), 1)
        qc, kc = _chunk_of(r, chunk), _chunk_of(c - 2 * bq, chunk)
        band = jnp.logical_and(kc <= qc, kc >= qc - left)
        for g in range(group):
            bias_s[g] = jnp.where(band, _toeplitz_tile(w_ref[g], bq, 3 * bq), NEG)

    for g in range(group):
        hs = slice(g * HEAD_LANES, (g + 1) * HEAD_LANES)
        q = q_ref[:, hs].astype(BF16)
        parts = []
        for j, kr in enumerate((k0, k1, k2)):
            pen = jnp.where(i - 2 + j >= 0, 0.0, NEG)
            parts.append(_dot_nt(q, kr[:, hs].astype(BF16)) * scale + bias_s[g, :, j * bq:(j + 1) * bq] + pen)
        ps = _softmax_parts(parts)
        o = functools.reduce(jnp.add, [_dot(p.astype(BF16), vr[:, hs].astype(BF16))
                                       for p, vr in zip(ps, (v0, v1, v2))])
        o_ref[:, hs] = o.astype(o_ref.dtype)


def attn_b_prompt(cfg, z, wband, layer, o_rows):
    B, S, H, bq, G = cfg.batch, cfg.seq, cfg.b_heads, cfg.bq_b, cfg.b_group
    nq = S // bq
    gw = G * HEAD_LANES
    c0 = 3 * cfg.a_width // gw
    ng = H // G

    def kv(col, j):
        return pl.BlockSpec((bq, gw), lambda b, hg, i: (b * nq + jnp.maximum(i - 2 + j, 0), c0 + col * ng + hg))

    return pl.pallas_call(
        functools.partial(_attn_b_prompt_kernel, bq=bq, group=G, chunk=cfg.chunk, left=cfg.b_left_chunks,
                          scale=HEAD_LANES ** -0.5),
        out_shape=jax.ShapeDtypeStruct(o_rows.shape, o_rows.dtype),
        grid=(B, ng, nq),
        in_specs=[pl.BlockSpec((bq, gw), lambda b, hg, i: (b * nq + i, c0 + hg)),
                  kv(1, 0), kv(1, 1), kv(1, 2), kv(2, 0), kv(2, 1), kv(2, 2),
                  pl.BlockSpec((None, G, 1, wband.shape[-1]), lambda b, hg, i: (layer, hg, 0, 0)),
                  pl.BlockSpec(memory_space=pl.ANY)],
        out_specs=pl.BlockSpec((bq, gw), lambda b, hg, i: (b * nq + i, hg)),
        scratch_shapes=[pltpu.VMEM((G, bq, 3 * bq), F32)],
        input_output_aliases={8: 0},
        compiler_params=_cparams(3, VMEM_LIMIT_V7X), name="attn_b_prompt",
    )(z, z, z, z, z, z, z, wband, o_rows)


def _attn_b_sample_kernel(q_ref, kn_ref, vn_ref, ck_ref, cv_ref, bias_ref, o_rows_ref, o_ref, *, heads, past, scale):
    del o_rows_ref
    for h in range(heads):
        hs = slice(h * HEAD_LANES, (h + 1) * HEAD_LANES)
        q = q_ref[:, hs].astype(BF16)
        ps = _softmax_parts([_dot_nt(q, ck_ref[0, h].astype(BF16)) * scale + bias_ref[h, :, 0:past],
                             _dot_nt(q, kn_ref[:, hs].astype(BF16)) * scale + bias_ref[h, :, past:]])
        o = (_dot(ps[0].astype(BF16), cv_ref[0, h].astype(BF16))
             + _dot(ps[1].astype(BF16), vn_ref[:, hs].astype(BF16)))
        o_ref[:, hs] = o.astype(o_ref.dtype)


def attn_b_sample(cfg, z, cache_k, cache_v, layer, bias, o_rows):
    Bd, T, H = cfg.dec_batch, cfg.dec_seq, cfg.b_heads
    Pc = cache_k.shape[3]
    W = cfg.b_width
    r0 = cfg.rows_p // T
    c0 = 3 * cfg.a_width // W
    new = lambda c: pl.BlockSpec((T, W), lambda b: (r0 + b, c0 + c))
    cache = pl.BlockSpec((None, 1, H, Pc, HEAD_LANES), lambda b: (layer, b, 0, 0, 0))
    return pl.pallas_call(
        functools.partial(_attn_b_sample_kernel, heads=H, past=Pc, scale=HEAD_LANES ** -0.5),
        out_shape=jax.ShapeDtypeStruct(o_rows.shape, o_rows.dtype),
        grid=(Bd,),
        in_specs=[new(0), new(1), new(2), cache, cache, pl.BlockSpec(bias.shape, lambda b: (0, 0, 0)),
                  pl.BlockSpec(memory_space=pl.ANY)],
        out_specs=pl.BlockSpec((T, W), lambda b: (r0 + b, 0)),
        input_output_aliases={6: 0},
        compiler_params=_cparams(1, VMEM_LIMIT_V7X), name="attn_b_sample",
    )(z, z, z, cache_k, cache_v, bias, o_rows)


def _split3(x):
    hi = x.astype(BF16)
    r = x - hi.astype(F32)
    mid = r.astype(BF16)
    lo = (r - mid.astype(F32)).astype(BF16)
    return hi, mid, lo


def _hgrn_head(qz, fz, vv, lb, st, L):
    sig = jax.nn.sigmoid(fz)
    logf = jnp.log(lb + (1.0 - lb) * sig)
    kk = (1.0 - lb) * (1.0 - sig)
    qq = _silu(qz)
    row = lax.broadcasted_iota(jnp.int32, (L, 1), 0)
    col = lax.broadcasted_iota(jnp.int32, (1, L), 1)
    tril = (col <= row).astype(BF16)
    b = functools.reduce(jnp.add, [_dot(tril, part) for part in _split3(logf)])
    vb = vv.astype(BF16)

    o = _dot_nt((qq * jnp.exp(b)).astype(BF16), st.astype(BF16))
    b_last = b[L - 1:L, :]
    kd = kk * jnp.exp(b_last - b)
    st_new = jnp.exp(b_last) * st + _dot_tn(vb, kd.astype(BF16))

    if L > SUB:
        p_off = jnp.zeros((L, L), F32)
        m = SUB
        while m < L:
            is_q = (row // m) % 2 == 1
            ref = b[m - 1:m, :]
            for g in range(1, L // (2 * m)):
                ref = jnp.where(row >= g * 2 * m, b[g * 2 * m + m - 1:g * 2 * m + m, :], ref)
            d = b - ref
            x = (jnp.where(is_q, qq, kk) * jnp.exp(jnp.where(is_q, d, -d))).astype(BF16)
            pair = jnp.logical_and(is_q, col // m == row // m - 1)
            p_off = p_off + jnp.where(pair, _dot_nt(x, x), 0.0)
            m *= 2
        o = o + _dot(p_off.astype(BF16), vb)

    ones = jnp.ones((HEAD_LANES, HEAD_LANES), BF16)
    t_idx = lax.broadcasted_iota(jnp.int32, (SUB, 1), 0)
    diag = []
    for r0 in range(0, L, SUB):
        bj, qj, kj, vj = (a[r0:r0 + SUB, :] for a in (b, qq, kk, vv))
        w = [jnp.exp(jnp.where(t_idx >= s, bj - bj[s:s + 1, :], NEG)) * qj * kj[s:s + 1, :] for s in range(SUB)]
        r = _dot(jnp.concatenate(w, axis=0).astype(BF16), ones)
        diag.append(functools.reduce(jnp.add, [r[s * SUB:(s + 1) * SUB, :] * vj[s:s + 1, :] for s in range(SUB)]))
    o = o + (jnp.concatenate(diag, axis=0) if len(diag) > 1 else diag[0])
    return o, st_new


def _hgrn_kernel(q_ref, f_ref, i_ref, g_ref, lb_ref, ng_ref, s0_ref, o_rows_ref, o_ref, so_ref, st_ref, *, heads, L):
    del o_rows_ref
    c = pl.program_id(1)

    @pl.when(c == 0)
    def _():
        st_ref[...] = s0_ref[0]

    for h in range(heads):
        hs = slice(h * HEAD_LANES, (h + 1) * HEAD_LANES)
        o, st_new = _hgrn_head(q_ref[:, hs], f_ref[:, hs], i_ref[:, hs], lb_ref[:, hs], st_ref[h], L)
        st_ref[h] = st_new
        o_ref[:, hs] = (_rms(o, ng_ref[...]) * _silu(g_ref[:, hs])).astype(o_ref.dtype)

    @pl.when(c == pl.num_programs(1) - 1)
    def _():
        so_ref[0] = st_ref[...]


def hgrn2(cfg, z, lb, norm_g, s0t, o_rows, *, nb, T, row0):
    H, W = cfg.c_heads, cfg.c_width
    L = min(cfg.chunk, T)
    nc = T // L
    c0 = (3 * cfg.a_width + 3 * cfg.b_width) // W
    r0 = row0 // L
    col = lambda c: pl.BlockSpec((L, W), lambda b, i: (r0 + b * nc + i, c0 + c))
    state = pl.BlockSpec((1, H, HEAD_LANES, HEAD_LANES), lambda b, i: (b, 0, 0, 0))
    return pl.pallas_call(
        functools.partial(_hgrn_kernel, heads=H, L=L),
        out_shape=[jax.ShapeDtypeStruct(o_rows.shape, o_rows.dtype),
                   jax.ShapeDtypeStruct((nb, H, HEAD_LANES, HEAD_LANES), F32)],
        grid=(nb, nc),
        in_specs=[col(0), col(1), col(2), col(3),
                  pl.BlockSpec((1, W), lambda b, i: (0, 0)),
                  pl.BlockSpec((1, HEAD_LANES), lambda b, i: (0, 0)),
                  state,
                  pl.BlockSpec(memory_space=pl.ANY)],
        out_specs=[pl.BlockSpec((L, W), lambda b, i: (r0 + b * nc + i, 0)), state],
        scratch_shapes=[pltpu.VMEM((H, HEAD_LANES, HEAD_LANES), F32)],
        input_output_aliases={7: 0},
        compiler_params=_cparams(2, VMEM_LIMIT_V7X), name="hgrn2",
    )(z, z, z, z, lb.reshape(1, W), norm_g.reshape(1, HEAD_LANES), s0t, o_rows)


def _t5_bucket(cfg, rel):
    half = cfg.t5_buckets // 2
    n = -rel
    ret = jnp.where(n < 0, half, 0)
    n = jnp.abs(n)
    max_exact = half // 2
    nf = jnp.maximum(n, 1).astype(F32)
    large = max_exact + (jnp.log(nf / max_exact) / math.log(cfg.t5_max_dist / max_exact)
                         * (half - max_exact)).astype(jnp.int32)
    large = jnp.minimum(large, half - 1)
    return ret + jnp.where(n < max_exact, n, large)


def _t5_tables(cfg, t5_table):
    bq = cfg.bq_a
    _, off = _toeplitz_offsets(bq, 2 * bq)
    near = t5_table[_t5_bucket(cfg, jnp.asarray(off - bq))].T
    far = t5_table[_t5_bucket(cfg, jnp.array([-2 * bq]))].T
    wt5 = ((near - far) * LOG2E)[:, None, :]
    sq = cfg.past_len + jnp.arange(cfg.dec_seq)
    sk = jnp.arange(cfg.past_len + cfg.dec_seq)
    sample = jnp.moveaxis(t5_table[_t5_bucket(cfg, sk[None, :] - sq[:, None])], -1, 0)
    return wt5.astype(F32), sample.astype(F32)


def _band_tables(cfg, rel_table, n_cache):
    bq = cfg.bq_b
    clipped = lambda dist: jnp.clip(dist, -cfg.b_max_dist, cfg.b_max_dist) + cfg.b_max_dist
    _, off = _toeplitz_offsets(bq, 3 * bq)
    wband = rel_table[..., clipped(jnp.asarray(2 * bq - off))][:, :, None, :]
    sq = cfg.past_len + jnp.arange(cfg.dec_seq)
    kbpos = jnp.concatenate([cfg.past_len - n_cache + jnp.arange(n_cache), sq])
    sample = rel_table[..., clipped(sq[:, None] - kbpos[None, :])]
    return wband.astype(F32), sample.astype(F32)


def _check(cfg):
    assert cfg.rows_s % 16 == 0 and cfg.seq % cfg.rows_s == 0
    assert cfg.rows % cfg.bm == 0 and cfg.bm % 16 == 0 and cfg.rows % cfg.bm_mix == 0 and cfg.bm_mix % 16 == 0
    assert cfg.d_model % cfg.bn == 0 and (cfg.n_in - 3 * cfg.d_model) % cfg.bn == 0 and cfg.n_in % cfg.bn == 0
    assert cfg.n_in % cfg.bn_wide == 0 and cfg.d_model % cfg.bn_wide == 0
    assert (3 * cfg.a_width + 3 * cfg.b_width) % cfg.c_width == 0 and (3 * cfg.a_width) % cfg.b_width == 0
    assert cfg.seq % cfg.bq_a == 0 and cfg.bq_a % cfg.chunk == 0 and cfg.bq_a >= cfg.t5_max_dist
    assert cfg.seq % cfg.bq_b == 0 and cfg.bq_b % cfg.chunk == 0 and 2 * cfg.bq_b >= cfg.b_win
    assert cfg.b_heads % cfg.b_group == 0 and (3 * cfg.a_width) % (cfg.b_group * HEAD_LANES) == 0
    assert cfg.chunk & (cfg.chunk - 1) == 0
    assert cfg.seq % cfg.chunk == 0 and cfg.chunk % SUB == 0 and cfg.dec_seq % SUB == 0
    assert cfg.past_len % cfg.chunk == 0 and cfg.dec_seq <= cfg.chunk
    assert min(cfg.b_win, cfg.past_len) <= cfg.b_win and cfg.rows_p % cfg.dec_seq == 0


def step(cfg, x_prompt, x_sample, cache_a_k, cache_a_v, cache_b_k, cache_b_v, state_c, c_prompt, c_sample,
         w_ada, b_ada, w_in, diff_lambda, a_subln_g, t5_bias, b_rel_bias, c_lb_param, c_norm_g,
         w_branch_a, w_branch_b, w_branch_c, w_o, ln1_g, ln1_b, ln2_g, ln2_b, w_up, w_down):
    _check(cfg)
    D, B, S, Bd, T = cfg.d_model, cfg.batch, cfg.seq, cfg.dec_batch, cfg.dec_seq
    RP = cfg.rows_p
    n_ctx = B + Bd
    pad = (-n_ctx) % 8
    c_all = jnp.concatenate([c_prompt, c_sample, jnp.zeros((pad, D), F32)], axis=0)
    mod = adaln_all(cfg, c_all, w_ada, b_ada)

    lb_soft = jax.nn.softmax(c_lb_param.astype(F32), axis=0)
    lb_all = jnp.cumsum(lb_soft, axis=0) - lb_soft[0]
    wt5, a_sample = _t5_tables(cfg, t5_bias)
    n_cache = cache_b_k.shape[2]
    wband, b_sample = _band_tables(cfg, b_rel_bias, n_cache)
    zero_state = jnp.zeros((B, cfg.c_heads, HEAD_LANES, HEAD_LANES), F32)
    A, Bw = cfg.a_width, cfg.b_width
    n_b_rows = min(cfg.b_win, S)
    cak, cav, cbk, cbv = (jnp.swapaxes(c, 2, 3) for c in (cache_a_k, cache_a_v, cache_b_k, cache_b_v))
    kv_stack = tuple(jnp.zeros((cfg.depth, B, cfg.a_heads, S, HEAD_LANES), F32) for _ in range(2))

    def heads(a, lead):
        return a.reshape(lead + (a.shape[-1] // HEAD_LANES, HEAD_LANES))

    x, h = rowwise(cfg, (x_prompt.reshape(RP, D), x_sample.reshape(Bd * T, D)), mod_next=mod[0], sc_col=1, sh_col=0)
    outs = [[] for _ in range(8)]
    for l in range(cfg.depth):
        lam_init = 0.8 - 0.6 * math.exp(-0.3 * l)
        z = matmul(cfg, h, w_in, l, bn=cfg.bn_wide, name="w_in")

        rows0 = lambda w: jnp.zeros((cfg.rows, w), BF16)
        oa, kv_stack = attn_a_prompt(cfg, z, wt5, diff_lambda[l], a_subln_g[l], lam_init, l, kv_stack, rows0(A))
        oa = attn_a_sample(cfg, z, cak, cav, l, a_sample, diff_lambda[l], a_subln_g[l], lam_init, oa)
        ob = attn_b_sample(cfg, z, cbk, cbv, l, b_sample[l], attn_b_prompt(cfg, z, wband, l, rows0(Bw)))
        oc, st_p = hgrn2(cfg, z, lb_all[l], c_norm_g[l], zero_state, rows0(cfg.c_width), nb=B, T=S, row0=0)
        oc, st_s = hgrn2(cfg, z, lb_all[l], c_norm_g[l], jnp.swapaxes(state_c[l], -1, -2), oc, nb=Bd, T=T, row0=RP)

        mixed = branch_mix(cfg, oa, ob, oc, z, w_branch_a, w_branch_b, w_branch_c, l)
        y = matmul(cfg, mixed, w_o, l, bn=cfg.bn_wide, out_dtype=BF16, name="w_o")
        x, h2 = rowwise(cfg, x, y, mod[l], ln1_g[l], ln1_b[l], mod[l], g_col=2, sc_col=4, sh_col=3)
        u = matmul(cfg, h2, w_up, l, bn=cfg.bn_wide, out_dtype=BF16, relu2=True, name="w_up")
        m = matmul(cfg, u, w_down, l, bn=cfg.bn, bk=D, out_dtype=BF16, name="w_down")
        if l + 1 < cfg.depth:
            x, h = rowwise(cfg, x, m, mod[l], ln2_g[l], ln2_b[l], mod[l + 1], g_col=5, sc_col=1, sh_col=0)
        else:
            x_p, x_s = rowwise(cfg, x, m, mod[l], ln2_g[l], ln2_b[l], split_out=True, g_col=5)

        kb0, vb0 = 3 * A + Bw, 3 * A + 2 * Bw
        tail = lambda c0: jnp.stack([z[(b + 1) * S - n_b_rows:(b + 1) * S, c0:c0 + Bw] for b in range(B)])
        outs[0].append(heads(tail(kb0), (B, n_b_rows)))
        outs[1].append(heads(tail(vb0), (B, n_b_rows)))
        outs[2].append(jnp.swapaxes(st_p, -1, -2))
        outs[3].append(heads(z[RP:, A:2 * A], (Bd, T)))
        outs[4].append(heads(z[RP:, 2 * A:3 * A], (Bd, T)))
        outs[5].append(heads(z[RP:, kb0:kb0 + Bw], (Bd, T)))
        outs[6].append(heads(z[RP:, vb0:vb0 + Bw], (Bd, T)))
        outs[7].append(jnp.swapaxes(st_s, -1, -2))

    new_a_kv = tuple(jnp.swapaxes(c, 2, 3) for c in kv_stack)
    return (x_p.reshape(B, S, D), x_s.reshape(Bd, T, D)) + new_a_kv + tuple(jnp.stack(o) for o in outs)


def kernel(x_prompt, x_sample, cache_a_k, cache_a_v, cache_b_k, cache_b_v, state_c, c_prompt, c_sample, w_ada, b_ada, w_in, diff_lambda, a_subln_g, t5_bias, b_rel_bias, c_lb_param, c_norm_g, w_branch_a, w_branch_b, w_branch_c, w_o, ln1_g, ln1_b, ln2_g, ln2_b, w_up, w_down):
    return step(Cfg(), x_prompt, x_sample, cache_a_k, cache_a_v, cache_b_k, cache_b_v, state_c, c_prompt, c_sample,
                w_ada, b_ada, w_in, diff_lambda, a_subln_g, t5_bias, b_rel_bias, c_lb_param, c_norm_g,
                w_branch_a, w_branch_b, w_branch_c, w_o, ln1_g, ln1_b, ln2_g, ln2_b, w_up, w_down)
```
